```python
import math
import jax
import jax.numpy as jnp
from jax import lax
import numpy as np

D_MODEL = 1024
BATCH = 2
SEQ = 8192
DEPTH = 1
DEC_BATCH = 32
DEC_SEQ = 8
PAST_LEN = 8192
PAGE_SIZE = 128

SB_HEADS = 8
SB_DH = 64
SB_W = SB_HEADS * SB_DH
DF_HEADS = 4
DF_DH = 64
DF_QK_W = DF_HEADS * 2 * DF_DH
DF_V_W = DF_HEADS * 2 * DF_DH
PROJ_SIZES = (SB_W, SB_W, SB_W, DF_QK_W, DF_QK_W, DF_V_W, D_MODEL, D_MODEL)
IN_W = sum(PROJ_SIZES)
N_BUCKETS = 32
MAX_DISTANCE = 128
N_EXPERTS = 32
TOP_K = 4
D_FF = D_MODEL
SWIGLU_LIMIT = 7.0
SWIGLU_ALPHA = 1.702
Q_BLOCK = 128
RMS_EPS = 1e-5
POOL_NUM = 5
POOL_DEN = 4

kernel_name = 'hybrid_stickbreak_diffattn_moe_step'


def rmsnorm(x, g):
    xf = x.astype(jnp.float32)
    y = xf * lax.rsqrt(jnp.mean(xf * xf, axis=-1, keepdims=True) + RMS_EPS)
    return (y * g.astype(jnp.float32)).astype(x.dtype)


def t5_bucket(q_pos, k_pos):
    n = jnp.maximum(q_pos[:, None] - k_pos[None, :], 0)
    max_exact = N_BUCKETS // 2
    nf = jnp.maximum(n, max_exact).astype(jnp.float32)
    large = max_exact + (jnp.log(nf / max_exact) / math.log(MAX_DISTANCE / max_exact)
                         * (N_BUCKETS - max_exact)).astype(jnp.int32)
    large = jnp.minimum(large, N_BUCKETS - 1)
    return jnp.where(n < max_exact, n, large)


def rel_bias_logits(table, q_pos, k_pos):
    return jnp.transpose(table[t5_bucket(q_pos, k_pos)], (2, 0, 1)).astype(jnp.float32)


def stick_breaking(q, k, v, q_pos, k_pos):
    z = jnp.einsum('bqhd,bkhd->bhqk', q, k).astype(jnp.float32) * (SB_DH ** -0.5)
    mask = k_pos[None, :] < q_pos[:, None]
    log_keep = jnp.where(mask, -jax.nn.softplus(z), 0.0)
    between = lax.cumsum(log_keep, axis=3, reverse=True) - log_keep
    a = jnp.where(mask, jnp.exp(jax.nn.log_sigmoid(z) + between), 0.0)
    return jnp.einsum('bhqk,bkhd->bqhd', a.astype(v.dtype), v)


def differential(q, k, v, bias, lam, q_pos, k_pos):
    s = jnp.einsum('bqhcd,bkhcd->bhcqk', q, k).astype(jnp.float32) * (DF_DH ** -0.5)
    s = s + bias[None, :, None]
    mask = k_pos[None, :] <= q_pos[:, None]
    p = jax.nn.softmax(jnp.where(mask, s, -jnp.inf), axis=-1)
    w = p[:, :, 0] - lam * p[:, :, 1]
    return jnp.einsum('bhqk,bkhe->bqhe', w.astype(v.dtype), v)


def sweep_query_blocks(fn, q, q_pos):
    b, t = q.shape[0], q.shape[1]
    if t % Q_BLOCK != 0:
        return fn(q, q_pos)
    nb = t // Q_BLOCK
    qb = jnp.moveaxis(q.reshape((b, nb, Q_BLOCK) + q.shape[2:]), 1, 0)
    pb = q_pos.reshape(nb, Q_BLOCK)
    out = lax.map(lambda qp: fn(qp[0], qp[1]), (qb, pb))
    out = jnp.moveaxis(out, 0, 1)
    return out.reshape((b, t) + out.shape[3:])


def gather_pages(pool, page_table):
    g = pool[page_table]
    return g.reshape((g.shape[0], g.shape[1] * g.shape[2]) + g.shape[3:])


def mixer_block(h, pos, past, rel_bias, lam, lambda_init, w_in, b_gate, g_subln, w_proj_a, w_proj_b, w_out):
    b, t, _ = h.shape
    offsets = [int(o) for o in np.cumsum(PROJ_SIZES)[:-1]]
    proj = jnp.einsum('btd,de->bte', h, w_in)
    sq, sk, sv, dq, dk, dv, ga, gb = jnp.split(proj, offsets, axis=-1)
    sq = sq.reshape(b, t, SB_HEADS, SB_DH)
    sk = sk.reshape(b, t, SB_HEADS, SB_DH)
    sv = sv.reshape(b, t, SB_HEADS, SB_DH)
    dq = dq.reshape(b, t, DF_HEADS, 2, DF_DH)
    dk = dk.reshape(b, t, DF_HEADS, 2, DF_DH)
    dv = dv.reshape(b, t, DF_HEADS, 2 * DF_DH)
    new_rows = (sk, sv, dk, dv)
    if past is None:
        ksb, vsb, kdf, vdf = new_rows
        k_pos = pos
    else:
        past_rows, past_pos = past
        ksb, vsb, kdf, vdf = [jnp.concatenate([pr, nr], axis=1) for pr, nr in zip(past_rows, new_rows)]
        k_pos = jnp.concatenate([past_pos, pos])
    sb = sweep_query_blocks(lambda qb, pb: stick_breaking(qb, ksb, vsb, pb, k_pos), sq, pos)
    df = sweep_query_blocks(
        lambda qb, pb: differential(qb, kdf, vdf, rel_bias_logits(rel_bias, pb, k_pos), lam, pb, k_pos), dq, pos)
    df = rmsnorm(df, g_subln) * (1.0 - lambda_init)
    branch_a = jnp.einsum('bte,ed->btd', sb.reshape(b, t, SB_W), w_proj_a)
    branch_b = jnp.einsum('bte,ed->btd', df.reshape(b, t, DF_V_W), w_proj_b)
    merged = jax.nn.sigmoid(ga + b_gate[0]) * branch_a + jax.nn.sigmoid(gb + b_gate[1]) * branch_b
    return jnp.einsum('btd,de->bte', merged, w_out), new_rows


def moe_ffn(h, w_router, b_router, w_exp_in, b_exp_in, w_exp_out, b_exp_out):
    b, t, d = h.shape
    xt = h.reshape(b * t, d)
    logits = (xt @ w_router + b_router).astype(jnp.float32)
    top_val, top_idx = lax.top_k(logits, TOP_K)
    gates = jax.nn.softmax(top_val, axis=-1)
    combine = jnp.einsum('nk,nke->ne', gates, jax.nn.one_hot(top_idx, N_EXPERTS, dtype=jnp.float32))
    y = jnp.zeros((b * t, d), jnp.float32)
    for e in range(N_EXPERTS):
        hid = xt @ w_exp_in[e] + b_exp_in[e]
        gate = jnp.minimum(hid[:, :D_FF], SWIGLU_LIMIT)
        up = jnp.clip(hid[:, D_FF:], -SWIGLU_LIMIT, SWIGLU_LIMIT)
        act = (up + 1.0) * gate * jax.nn.sigmoid(SWIGLU_ALPHA * gate)
        y = y + combine[:, e:e + 1] * (act @ w_exp_out[e] + b_exp_out[e])
    return y.astype(h.dtype).reshape(b, t, d)


def decoder_layer(x, pos, past, rel_bias, lam, lambda_init, g_attn, w_in, b_gate, g_subln, w_proj_a, w_proj_b,
                  w_out, g_ffn, w_router, b_router, w_exp_in, b_exp_in, w_exp_out, b_exp_out):
    mix, rows = mixer_block(rmsnorm(x, g_attn), pos, past, rel_bias, lam, lambda_init, w_in, b_gate, g_subln,
                            w_proj_a, w_proj_b, w_out)
    x = x + mix
    x = x + moe_ffn(rmsnorm(x, g_ffn), w_router, b_router, w_exp_in, b_exp_in, w_exp_out, b_exp_out)
    return x, rows


def setup_inputs(seed: int = 0) -> dict:
    key = jax.random.key(seed)
    ks = jax.random.split(key, 32)
    n_pages = PAST_LEN // PAGE_SIZE
    n_used = DEC_BATCH * n_pages
    n_pool = (n_used * POOL_NUM) // POOL_DEN

    def nrm(k, shape, scale):
        return jax.random.normal(k, shape, jnp.float32) * scale

    def gain(k, shape):
        return 1.0 + nrm(k, shape, 0.02)

    page_table = jax.random.permutation(ks[0], n_pool)[:n_used].reshape(DEC_BATCH, n_pages).astype(jnp.int32)
    return {
        'x_prompt': nrm(ks[1], (BATCH, SEQ, D_MODEL), 1.0),
        'x_sample': nrm(ks[2], (DEC_BATCH, DEC_SEQ, D_MODEL), 1.0),
        'cache_sb_k': nrm(ks[3], (DEPTH, n_pool, PAGE_SIZE, SB_HEADS, SB_DH), 1.0),
        'cache_sb_v': nrm(ks[4], (DEPTH, n_pool, PAGE_SIZE, SB_HEADS, SB_DH), 1.0),
        'cache_df_k': nrm(ks[5], (DEPTH, n_pool, PAGE_SIZE, DF_HEADS, 2, DF_DH), 1.0),
        'cache_df_v': nrm(ks[6], (DEPTH, n_pool, PAGE_SIZE, DF_HEADS, 2 * DF_DH), 1.0),
        'page_table': page_table,
        'rel_bias': nrm(ks[7], (N_BUCKETS, DF_HEADS), 0.5),
        'g_attn': gain(ks[8], (DEPTH, D_MODEL)),
        'w_in': nrm(ks[9], (DEPTH, D_MODEL, IN_W), D_MODEL ** -0.5),
        'b_gate': nrm(ks[10], (DEPTH, 2, D_MODEL), 0.1),
        'df_lambda_q1': nrm(ks[11], (DEPTH, DF_DH), 0.1),
        'df_lambda_k1': nrm(ks[12], (DEPTH, DF_DH), 0.1),
        'df_lambda_q2': nrm(ks[13], (DEPTH, DF_DH), 0.1),
        'df_lambda_k2': nrm(ks[14], (DEPTH, DF_DH), 0.1),
        'g_subln': gain(ks[15], (DEPTH, 2 * DF_DH)),
        'w_proj_a': nrm(ks[16], (DEPTH, SB_W, D_MODEL), SB_W ** -0.5),
        'w_proj_b': nrm(ks[17], (DEPTH, DF_V_W, D_MODEL), DF_V_W ** -0.5),
        'w_out': nrm(ks[18], (DEPTH, D_MODEL, D_MODEL), D_MODEL ** -0.5),
        'g_ffn': gain(ks[19], (DEPTH, D_MODEL)),
        'w_router': nrm(ks[20], (DEPTH, D_MODEL, N_EXPERTS), D_MODEL ** -0.5),
        'b_router': nrm(ks[21], (DEPTH, N_EXPERTS), 0.01),
        'w_exp_in': nrm(ks[22], (DEPTH, N_EXPERTS, D_MODEL, 2 * D_FF), D_MODEL ** -0.5),
        'b_exp_in': nrm(ks[23], (DEPTH, N_EXPERTS, 2 * D_FF), 0.02),
        'w_exp_out': nrm(ks[24], (DEPTH, N_EXPERTS, D_FF, D_MODEL), D_FF ** -0.5),
        'b_exp_out': nrm(ks[25], (DEPTH, N_EXPERTS, D_MODEL), 0.02),
        'g_final': gain(ks[26], (D_MODEL,)),
    }


def reference(x_prompt, x_sample, cache_sb_k, cache_sb_v, cache_df_k, cache_df_v, page_table,
              rel_bias, g_attn, w_in, b_gate, df_lambda_q1, df_lambda_k1, df_lambda_q2, df_lambda_k2,
              g_subln, w_proj_a, w_proj_b, w_out, g_ffn, w_router, b_router,
              w_exp_in, b_exp_in, w_exp_out, b_exp_out, g_final):
    past_len = page_table.shape[1] * cache_sb_k.shape[2]
    pos_p = jnp.arange(x_prompt.shape[1], dtype=jnp.int32)
    pos_s = past_len + jnp.arange(x_sample.shape[1], dtype=jnp.int32)
    pos_past = jnp.arange(past_len, dtype=jnp.int32)
    xp, xs = x_prompt, x_sample
    rows_p, rows_s = [], []
    for l in range(DEPTH):
        lambda_init = 0.8 - 0.6 * math.exp(-0.3 * l)
        lam = (jnp.exp(jnp.sum(df_lambda_q1[l].astype(jnp.float32) * df_lambda_k1[l].astype(jnp.float32)))
               - jnp.exp(jnp.sum(df_lambda_q2[l].astype(jnp.float32) * df_lambda_k2[l].astype(jnp.float32)))
               + lambda_init)
        params = (g_attn[l], w_in[l], b_gate[l], g_subln[l], w_proj_a[l], w_proj_b[l], w_out[l],
                  g_ffn[l], w_router[l], b_router[l], w_exp_in[l], b_exp_in[l], w_exp_out[l], b_exp_out[l])
        past_rows = tuple(gather_pages(c[l], page_table) for c in (cache_sb_k, cache_sb_v, cache_df_k, cache_df_v))
        xp, rp = decoder_layer(xp, pos_p, None, rel_bias, lam, lambda_init, *params)
        xs, rs = decoder_layer(xs, pos_s, (past_rows, pos_past), rel_bias, lam, lambda_init, *params)
        rows_p.append(rp)
        rows_s.append(rs)
    y_prompt = rmsnorm(xp, g_final)
    y_sample = rmsnorm(xs, g_final)
    sb_k_p = jnp.stack([r[0] for r in rows_p])
    sb_v_p = jnp.stack([r[1] for r in rows_p])
    df_k_p = jnp.stack([r[2] for r in rows_p])
    df_v_p = jnp.stack([r[3] for r in rows_p])
    sb_k_s = jnp.stack([r[0] for r in rows_s])
    sb_v_s = jnp.stack([r[1] for r in rows_s])
    df_k_s = jnp.stack([r[2] for r in rows_s])
    df_v_s = jnp.stack([r[3] for r in rows_s])
    return (y_prompt, y_sample, sb_k_p, sb_v_p, df_k_p, df_v_p, sb_k_s, sb_v_s, df_k_s, df_v_s)
```

```python
import functools
import math

import jax
import jax.numpy as jnp
from jax import lax
from jax.experimental import pallas as pl
from jax.experimental.pallas import tpu as pltpu

F32 = jnp.float32
BF16 = jnp.bfloat16

D_MODEL = 1024
SB_HEADS = 8
DF_HEADS = 4
HEAD_DIM = 64
BRANCH_W = 512
LANES = 128
N_BUCKETS = 32
MAX_DISTANCE = 128
N_EXPERTS = 32
TOP_K = 4
D_FF = 1024
SWIGLU_LIMIT = 7.0
SWIGLU_ALPHA = 1.702
RMS_EPS = 1e-5
QK_SCALE = HEAD_DIM ** -0.5

QKV_W = 6 * BRANCH_W
COL_SQ, COL_SK, COL_SV, COL_DQ, COL_DK, COL_DV = (i * (BRANCH_W // LANES) for i in range(6))

SB_STOP = 110.0
NEG_BIG = -1e30
VMEM_LIMIT = 56 * 1024 * 1024


def _nt_dot(a, b):
    return lax.dot_general(a, b, (((1,), (1,)), ((), ())), preferred_element_type=F32)


def _dot(a, b):
    return jnp.dot(a, b, preferred_element_type=F32)


def _split_dot(x, w_bf16):
    hi = x.astype(BF16)
    lo = (x - hi.astype(F32)).astype(BF16)
    return _dot(hi, w_bf16) + _dot(lo, w_bf16)


def _softplus(z):
    return jnp.maximum(z, 0.0) + jnp.log1p(jnp.exp(-jnp.abs(z)))


def _rms(x, g):
    return x * lax.rsqrt(jnp.mean(x * x, axis=-1, keepdims=True) + RMS_EPS) * g


def _inproj_kernel(x_ref, g_ref, w_ref, qkv_ref, sk_ref, sv_ref, dk_ref, dv_ref, gate_ref):
    h = _rms(x_ref[...], g_ref[...]).astype(BF16)

    def sec(c):
        return _dot(h, w_ref[:, c * BRANCH_W:(c + 1) * BRANCH_W])

    for c, (scale, f32_out) in enumerate(((QK_SCALE, None), (1.0, sk_ref), (1.0, sv_ref),
                                          (QK_SCALE, None), (1.0, dk_ref), (1.0, dv_ref))):
        p = sec(c)
        if f32_out is not None:
            f32_out[...] = p
        qkv_ref[:, c * BRANCH_W:(c + 1) * BRANCH_W] = (p * scale).astype(BF16)
    for c in range(4):
        gate_ref[:, c * BRANCH_W:(c + 1) * BRANCH_W] = sec(6 + c)


def _inproj(x, g, w_bf16, tm):
    n = x.shape[0]
    in_w = w_bf16.shape[1]
    kv = jax.ShapeDtypeStruct((n, BRANCH_W), F32)
    row = lambda i: (i, 0)
    return pl.pallas_call(
        _inproj_kernel,
        grid=(n // tm,),
        in_specs=[pl.BlockSpec((tm, D_MODEL), row),
                  pl.BlockSpec((1, D_MODEL), lambda i: (0, 0)),
                  pl.BlockSpec((D_MODEL, in_w), lambda i: (0, 0))],
        out_specs=[pl.BlockSpec((tm, QKV_W), row)] + [pl.BlockSpec((tm, BRANCH_W), row)] * 4
                  + [pl.BlockSpec((tm, 2 * D_MODEL), row)],
        out_shape=[jax.ShapeDtypeStruct((n, QKV_W), BF16), kv, kv, kv, kv,
                   jax.ShapeDtypeStruct((n, 2 * D_MODEL), F32)],
        compiler_params=pltpu.CompilerParams(dimension_semantics=("parallel",),
                                             vmem_limit_bytes=VMEM_LIMIT),
        name="inproj",
    )(x, g, w_bf16)


def _half_masks():
    lane = lax.broadcasted_iota(jnp.int32, (1, LANES), 1)
    return lane < HEAD_DIM, lane >= HEAD_DIM


def _stack_halves(q):
    lo, hi = _half_masks()
    zero = jnp.zeros_like(q)
    return jnp.concatenate([jnp.where(lo, q, zero), jnp.where(hi, q, zero)], axis=0)


def _strict_upper_ones(n):
    j = lax.broadcasted_iota(jnp.int32, (n, n), 0)
    s = lax.broadcasted_iota(jnp.int32, (n, n), 1)
    return jnp.where(j > s, 1.0, 0.0).astype(BF16)


def _sb_block(q2, kb, vb, tri, mask, carry):
    z = _nt_dot(q2, kb)
    sp = _softplus(z)
    log_keep = -sp if mask is None else jnp.where(mask, -sp, 0.0)
    between = _split_dot(log_keep, tri)
    a = jnp.exp(z - sp + between + carry)
    if mask is not None:
        a = jnp.where(mask, a, 0.0)
    return _dot(a.astype(BF16), vb), carry + jnp.sum(log_keep, axis=1, keepdims=True)


def _sb_attn_kernel(q_ref, k_ref, v_ref, o_ref, acc_ref, car_ref, *, tq):
    i = pl.program_id(2)
    q2 = _stack_halves(q_ref[...])
    tri = _strict_upper_ones(tq)
    row = lax.broadcasted_iota(jnp.int32, (2 * tq, tq), 0)
    row = jnp.where(row >= tq, row - tq, row)
    col = lax.broadcasted_iota(jnp.int32, (2 * tq, tq), 1)
    acc_ref[...] = jnp.zeros_like(acc_ref)
    car_ref[...] = jnp.zeros_like(car_ref)

    def body(state):
        j, _ = state
        ks = pl.multiple_of(j * tq, tq)
        mask = (col + (j - i) * tq) < row
        pv, carry = _sb_block(q2, k_ref[pl.ds(ks, tq), :], v_ref[pl.ds(ks, tq), :], tri, mask, car_ref[...])
        acc_ref[...] += pv
        car_ref[...] = carry
        return j - 1, jnp.max(carry)

    lax.while_loop(lambda s: (s[0] >= 0) & (s[1] > -SB_STOP), body, (i, jnp.float32(0.0)))
    lo, _ = _half_masks()
    o_ref[...] = jnp.where(lo, acc_ref[:tq, :], acc_ref[tq:, :]).astype(o_ref.dtype)


def _sb_attn(qkv, batch, seq, tq):
    nq = seq // tq
    return pl.pallas_call(
        functools.partial(_sb_attn_kernel, tq=tq),
        grid=(batch, BRANCH_W // LANES, nq),
        in_specs=[pl.BlockSpec((tq, LANES), lambda b, p, i: (b * nq + i, COL_SQ + p)),
                  pl.BlockSpec((seq, LANES), lambda b, p, i: (b, COL_SK + p)),
                  pl.BlockSpec((seq, LANES), lambda b, p, i: (b, COL_SV + p))],
        out_specs=pl.BlockSpec((tq, LANES), lambda b, p, i: (b * nq + i, p)),
        out_shape=jax.ShapeDtypeStruct((batch * seq, BRANCH_W), BF16),
        scratch_shapes=[pltpu.VMEM((2 * tq, LANES), F32), pltpu.VMEM((2 * tq, 1), F32)],
        compiler_params=pltpu.CompilerParams(dimension_semantics=("parallel", "parallel", "arbitrary"),
                                             vmem_limit_bytes=VMEM_LIMIT),
        name="sb_attn",
    )(qkv, qkv, qkv)


def _lambda(lamv_ref, lambda_init):
    a = jnp.sum(lamv_ref[0:1, :] * lamv_ref[1:2, :], axis=1, keepdims=True)
    b = jnp.sum(lamv_ref[2:3, :] * lamv_ref[3:4, :], axis=1, keepdims=True)
    return jnp.exp(a) - jnp.exp(b) + lambda_init


def _softmax_block(q2, kb, vb, bias, mask, m, l, acc):
    s = _nt_dot(q2, kb)
    if bias is not None:
        s = s + bias
    if mask is not None:
        s = jnp.where(mask, s, NEG_BIG)
    m_new = jnp.maximum(m, jnp.max(s, axis=1, keepdims=True))
    alpha = jnp.exp(m - m_new)
    p = jnp.exp(s - m_new)
    l_new = alpha * l + jnp.sum(p, axis=1, keepdims=True)
    return m_new, l_new, alpha * acc + _dot(p.astype(BF16), vb)


def _df_attn_kernel(lamv_ref, q_ref, k_ref, v_ref, bias_ref, g_ref, o_ref, m_ref, l_ref, acc_ref,
                    *, tq, lambda_init):
    i = pl.program_id(2)
    q2 = _stack_halves(q_ref[...])
    m_ref[...] = jnp.full_like(m_ref, NEG_BIG)
    l_ref[...] = jnp.zeros_like(l_ref)
    acc_ref[...] = jnp.zeros_like(acc_ref)

    def step(j, bias, mask):
        ks = pl.multiple_of(j * tq, tq)
        if bias is not None:
            bias = jnp.concatenate([bias, bias], axis=0)
        m, l, acc = _softmax_block(q2, k_ref[pl.ds(ks, tq), :], v_ref[pl.ds(ks, tq), :], bias, mask,
                                   m_ref[...], l_ref[...], acc_ref[...])
        m_ref[...] = m
        l_ref[...] = l
        acc_ref[...] = acc

    def far(j, c):
        step(j, None, None)
        return c

    lax.fori_loop(0, jnp.maximum(i - 1, 0), far, 0)

    @pl.when(i >= 1)
    def _():
        step(i - 1, bias_ref[0, 1], None)

    row = lax.broadcasted_iota(jnp.int32, (2 * tq, tq), 0)
    row = jnp.where(row >= tq, row - tq, row)
    col = lax.broadcasted_iota(jnp.int32, (2 * tq, tq), 1)
    step(i, bias_ref[0, 0], col <= row)

    lam = _lambda(lamv_ref, lambda_init)
    norm = acc_ref[...] / l_ref[...]
    out = norm[:tq, :] - lam * norm[tq:, :]
    o_ref[...] = (_rms(out, g_ref[...]) * (1.0 - lambda_init)).astype(o_ref.dtype)


def _df_attn(qkv, lamv, bias_tiles, g_subln, batch, seq, tq, lambda_init):
    nq = seq // tq
    return pl.pallas_call(
        functools.partial(_df_attn_kernel, tq=tq, lambda_init=lambda_init),
        grid=(batch, DF_HEADS, nq),
        in_specs=[pl.BlockSpec((4, HEAD_DIM), lambda b, h, i: (0, 0)),
                  pl.BlockSpec((tq, LANES), lambda b, h, i: (b * nq + i, COL_DQ + h)),
                  pl.BlockSpec((seq, LANES), lambda b, h, i: (b, COL_DK + h)),
                  pl.BlockSpec((seq, LANES), lambda b, h, i: (b, COL_DV + h)),
                  pl.BlockSpec((1, 2, tq, tq), lambda b, h, i: (h, 0, 0, 0)),
                  pl.BlockSpec((1, LANES), lambda b, h, i: (0, 0))],
        out_specs=pl.BlockSpec((tq, LANES), lambda b, h, i: (b * nq + i, h)),
        out_shape=jax.ShapeDtypeStruct((batch * seq, BRANCH_W), BF16),
        scratch_shapes=[pltpu.VMEM((2 * tq, 1), F32), pltpu.VMEM((2 * tq, 1), F32),
                        pltpu.VMEM((2 * tq, LANES), F32)],
        compiler_params=pltpu.CompilerParams(dimension_semantics=("parallel", "parallel", "arbitrary"),
                                             vmem_limit_bytes=VMEM_LIMIT),
        name="df_attn",
    )(lamv, qkv, qkv, qkv, bias_tiles, g_subln)


def _decode_kernel(pt_ref, lamv_ref, qkv_ref, ksb_ref, vsb_ref, kdf_ref, vdf_ref, bias_ref, g_ref,
                   osb_ref, odf_ref, qsb_s, qdf_s, acc_sb, car_s, acc_df, m_s, l_s,
                   *, dq, page, lambda_init):
    s = pl.program_id(1)
    n_steps = pl.num_programs(1)
    rows = SB_HEADS * dq
    row = lax.broadcasted_iota(jnp.int32, (rows, page), 0)
    col = lax.broadcasted_iota(jnp.int32, (rows, page), 1)
    r_in = row % dq
    tri = _strict_upper_ones(page)

    def sb_step(kb, vb, mask):
        pv, carry = _sb_block(qsb_s[...], kb, vb, tri, mask, car_s[...])
        acc_sb[...] += pv
        car_s[...] = carry

    def df_step(kb, vb, bias, mask):
        m, l, acc = _softmax_block(qdf_s[...], kb, vb, bias, mask, m_s[...], l_s[...], acc_df[...])
        m_s[...] = m
        l_s[...] = l
        acc_df[...] = acc

    @pl.when(s == 0)
    def _():
        rq = lax.broadcasted_iota(jnp.int32, (rows, BRANCH_W), 0) // dq
        lq = lax.broadcasted_iota(jnp.int32, (rows, BRANCH_W), 1) // HEAD_DIM
        for c, dst in ((0, qsb_s), (3, qdf_s)):
            q = qkv_ref[0, :, c * BRANCH_W:(c + 1) * BRANCH_W].astype(F32)
            qt = jnp.concatenate([q] * SB_HEADS, axis=0)
            dst[...] = jnp.where(rq == lq, qt, 0.0).astype(BF16)
        acc_sb[...] = jnp.zeros_like(acc_sb)
        car_s[...] = jnp.zeros_like(car_s)
        acc_df[...] = jnp.zeros_like(acc_df)
        m_s[...] = jnp.full_like(m_s, NEG_BIG)
        l_s[...] = jnp.zeros_like(l_s)

        def new_rows(c):
            x = qkv_ref[0, :, c * BRANCH_W:(c + 1) * BRANCH_W].astype(F32)
            return jnp.concatenate([x, jnp.zeros((page - dq, BRANCH_W), F32)], axis=0).astype(BF16)

        sb_step(new_rows(1), new_rows(2), col < r_in)
        df_step(new_rows(4), new_rows(5), bias_ref[0], col <= r_in)

    def cached(c_ref):
        return c_ref[0].astype(BF16)

    @pl.when(s >= 1)
    def _():
        @pl.when(jnp.max(car_s[...]) > -SB_STOP)
        def _():
            sb_step(cached(ksb_ref), cached(vsb_ref), None)

    @pl.when(s == 1)
    def _():
        df_step(cached(kdf_ref), cached(vdf_ref), bias_ref[1], None)

    @pl.when(s >= 2)
    def _():
        df_step(cached(kdf_ref), cached(vdf_ref), None, None)

    @pl.when(s == n_steps - 1)
    def _():
        rq = lax.broadcasted_iota(jnp.int32, (rows, BRANCH_W), 0)
        lq = lax.broadcasted_iota(jnp.int32, (rows, BRANCH_W), 1)

        def fold(x):
            return functools.reduce(lambda a, b: a + b, [x[g * dq:(g + 1) * dq, :] for g in range(SB_HEADS)])

        sb = fold(jnp.where(rq // dq == lq // HEAD_DIM, acc_sb[...], 0.0))
        osb_ref[0] = sb.astype(osb_ref.dtype)

        lam = _lambda(lamv_ref, lambda_init)
        is_map1 = (lax.broadcasted_iota(jnp.int32, (rows, 1), 0) // dq) % 2 == 1
        scale = jnp.where(is_map1, -lam, 1.0) / l_s[...]
        df = fold(jnp.where(rq // (2 * dq) == lq // LANES, acc_df[...] * scale, 0.0))
        g = g_ref[...]
        df = jnp.concatenate([_rms(df[:, h * LANES:(h + 1) * LANES], g) for h in range(DF_HEADS)], axis=1)
        odf_ref[0] = (df * (1.0 - lambda_init)).astype(odf_ref.dtype)


def _decode(page_table, lamv, qkv3, caches, dec_bias, g_subln, lambda_init):
    nb, dq, _ = qkv3.shape
    n_pages = page_table.shape[1]
    page = caches[0].shape[1]
    rows = SB_HEADS * dq

    def cache_map(b, s, pt):
        return (pt[b, n_pages - 1 - jnp.maximum(s - 1, 0)], 0, 0)

    cache_spec = pl.BlockSpec((1, page, BRANCH_W), cache_map)
    out = jax.ShapeDtypeStruct((nb, dq, BRANCH_W), BF16)
    grid_spec = pltpu.PrefetchScalarGridSpec(
        num_scalar_prefetch=1,
        grid=(nb, n_pages + 1),
        in_specs=[pl.BlockSpec((4, HEAD_DIM), lambda b, s, pt: (0, 0)),
                  pl.BlockSpec((1, dq, QKV_W), lambda b, s, pt: (b, 0, 0)),
                  cache_spec, cache_spec, cache_spec, cache_spec,
                  pl.BlockSpec((2, rows, page), lambda b, s, pt: (0, 0, 0)),
                  pl.BlockSpec((1, LANES), lambda b, s, pt: (0, 0))],
        out_specs=[pl.BlockSpec((1, dq, BRANCH_W), lambda b, s, pt: (b, 0, 0))] * 2,
        scratch_shapes=[pltpu.VMEM((rows, BRANCH_W), BF16), pltpu.VMEM((rows, BRANCH_W), BF16),
                        pltpu.VMEM((rows, BRANCH_W), F32), pltpu.VMEM((rows, 1), F32),
                        pltpu.VMEM((rows, BRANCH_W), F32), pltpu.VMEM((rows, 1), F32),
                        pltpu.VMEM((rows, 1), F32)],
    )
    return pl.pallas_call(
        functools.partial(_decode_kernel, dq=dq, page=page, lambda_init=lambda_init),
        grid_spec=grid_spec,
        out_shape=[out, out],
        compiler_params=pltpu.CompilerParams(dimension_semantics=("parallel", "arbitrary"),
                                             vmem_limit_bytes=VMEM_LIMIT),
        name="decode_attn",
    )(page_table, lamv, qkv3, *caches, dec_bias, g_subln)


def _post_kernel(sb_ref, df_ref, ga_ref, gb_ref, x_ref, bg_ref, wa_ref, wb_ref, wo_ref, gf_ref,
                 wr_ref, br_ref, x1_ref, h2_ref, comb_ref):
    a = _dot(sb_ref[...], wa_ref[...])
    b = _dot(df_ref[...], wb_ref[...])
    merged = jax.nn.sigmoid(ga_ref[...] + bg_ref[0:1, :]) * a + jax.nn.sigmoid(gb_ref[...] + bg_ref[1:2, :]) * b
    x1 = x_ref[...] + _dot(merged.astype(BF16), wo_ref[...])
    x1_ref[...] = x1
    h2 = _rms(x1, gf_ref[...])
    h2_ref[...] = h2.astype(BF16)

    wr = wr_ref[...]
    wr_hi = wr.astype(BF16)
    wr_lo = (wr - wr_hi.astype(F32)).astype(BF16)
    h_hi = h2.astype(BF16)
    h_lo = (h2 - h_hi.astype(F32)).astype(BF16)
    logits = _dot(h_hi, wr_hi) + _dot(h_hi, wr_lo) + _dot(h_lo, wr_hi) + br_ref[...]

    lane = lax.broadcasted_iota(jnp.int32, logits.shape, 1)
    vals, hots = [], []
    for _ in range(TOP_K):
        mx = jnp.max(logits, axis=1, keepdims=True)
        first = jnp.min(jnp.where(logits == mx, lane, N_EXPERTS), axis=1, keepdims=True)
        hot = lane == first
        vals.append(mx)
        hots.append(hot)
        logits = jnp.where(hot, -jnp.inf, logits)
    es = [jnp.exp(v - vals[0]) for v in vals]
    denom = functools.reduce(lambda u, v: u + v, es)
    comb = jnp.zeros(logits.shape, F32)
    for e, hot in zip(es, hots):
        comb = comb + jnp.where(hot, e / denom, 0.0)
    comb_ref[...] = comb


def _post(sb, df, gates, x, b_gate, wa, wb, wo, g_ffn, w_router, b_router, tm):
    n = x.shape[0]
    row = lambda i: (i, 0)
    const = lambda i: (0, 0)
    return pl.pallas_call(
        _post_kernel,
        grid=(n // tm,),
        in_specs=[pl.BlockSpec((tm, BRANCH_W), row), pl.BlockSpec((tm, BRANCH_W), row),
                  pl.BlockSpec((tm, D_MODEL), lambda i: (i, 0)), pl.BlockSpec((tm, D_MODEL), lambda i: (i, 1)),
                  pl.BlockSpec((tm, D_MODEL), row),
                  pl.BlockSpec((2, D_MODEL), const),
                  pl.BlockSpec((BRANCH_W, D_MODEL), const), pl.BlockSpec((BRANCH_W, D_MODEL), const),
                  pl.BlockSpec((D_MODEL, D_MODEL), const),
                  pl.BlockSpec((1, D_MODEL), const),
                  pl.BlockSpec((D_MODEL, N_EXPERTS), const), pl.BlockSpec((1, N_EXPERTS), const)],
        out_specs=[pl.BlockSpec((tm, D_MODEL), row), pl.BlockSpec((tm, D_MODEL), row),
                   pl.BlockSpec((tm, N_EXPERTS), row)],
        out_shape=[jax.ShapeDtypeStruct((n, D_MODEL), F32), jax.ShapeDtypeStruct((n, D_MODEL), BF16),
                   jax.ShapeDtypeStruct((n, N_EXPERTS), F32)],
        compiler_params=pltpu.CompilerParams(dimension_semantics=("parallel",),
                                             vmem_limit_bytes=VMEM_LIMIT),
        name="post_attn",
    )(sb, df, gates, gates, x, b_gate, wa, wb, wo, g_ffn, w_router, b_router)


def _moe_kernel(h_ref, comb_ref, x1_ref, wi_ref, bi_ref, wo_ref, bo_ref, gfin_ref, y_ref, acc_ref):
    e = pl.program_id(1)

    @pl.when(e == 0)
    def _():
        acc_ref[...] = jnp.zeros_like(acc_ref)

    h = h_ref[...]
    gate = jnp.minimum(_dot(h, wi_ref[0, :, :D_FF]) + bi_ref[0, :, :D_FF], SWIGLU_LIMIT)
    up = jnp.clip(_dot(h, wi_ref[0, :, D_FF:]) + bi_ref[0, :, D_FF:], -SWIGLU_LIMIT, SWIGLU_LIMIT)
    act = (up + 1.0) * gate * jax.nn.sigmoid(SWIGLU_ALPHA * gate)
    out = _dot(act.astype(BF16), wo_ref[0]) + bo_ref[0]
    comb = comb_ref[...]
    lane = lax.broadcasted_iota(jnp.int32, comb.shape, 1)
    w = jnp.sum(jnp.where(lane == e, comb, 0.0), axis=1, keepdims=True)
    acc_ref[...] += w * out

    @pl.when(e == pl.num_programs(1) - 1)
    def _():
        y_ref[...] = _rms(x1_ref[...] + acc_ref[...], gfin_ref[...])


def _moe(h2, comb, x1, wi, bi, wo, bo, g_final, tm):
    n = h2.shape[0]
    row = lambda i, e: (i, 0)
    exp3 = lambda i, e: (e, 0, 0)
    return pl.pallas_call(
        _moe_kernel,
        grid=(n // tm, N_EXPERTS),
        in_specs=[pl.BlockSpec((tm, D_MODEL), row), pl.BlockSpec((tm, N_EXPERTS), row),
                  pl.BlockSpec((tm, D_MODEL), row),
                  pl.BlockSpec((1, D_MODEL, 2 * D_FF), exp3), pl.BlockSpec((1, 1, 2 * D_FF), exp3),
                  pl.BlockSpec((1, D_FF, D_MODEL), exp3), pl.BlockSpec((1, 1, D_MODEL), exp3),
                  pl.BlockSpec((1, D_MODEL), lambda i, e: (0, 0))],
        out_specs=pl.BlockSpec((tm, D_MODEL), row),
        out_shape=jax.ShapeDtypeStruct((n, D_MODEL), F32),
        scratch_shapes=[pltpu.VMEM((tm, D_MODEL), F32)],
        compiler_params=pltpu.CompilerParams(dimension_semantics=("parallel", "arbitrary"),
                                             vmem_limit_bytes=VMEM_LIMIT),
        name="moe",
    )(h2, comb, x1, wi, bi, wo, bo, g_final)


def _bucket(n):
    max_exact = N_BUCKETS // 2
    nf = jnp.maximum(n, max_exact).astype(F32)
    large = max_exact + (jnp.log(nf / max_exact) / math.log(MAX_DISTANCE / max_exact)
                         * (N_BUCKETS - max_exact)).astype(jnp.int32)
    return jnp.where(n < max_exact, n, jnp.minimum(large, N_BUCKETS - 1))


def _tile(n, cap):
    t = cap
    while n % t:
        t //= 2
    return t


def kernel(x_prompt, x_sample, cache_sb_k, cache_sb_v, cache_df_k, cache_df_v, page_table, rel_bias, g_attn, w_in, b_gate, df_lambda_q1, df_lambda_k1, df_lambda_q2, df_lambda_k2, g_subln, w_proj_a, w_proj_b, w_out, g_ffn, w_router, b_router, w_exp_in, b_exp_in, w_exp_out, b_exp_out, g_final):
    depth = w_in.shape[0]
    assert depth == 1, "single-layer step"
    batch, seq, _ = x_prompt.shape
    nb, dq, _ = x_sample.shape
    n_pool, page = cache_sb_k.shape[1], cache_sb_k.shape[2]
    n_pages = page_table.shape[1]
    past_len = n_pages * page
    tq = _tile(seq, 256)
    lambda_init = 0.8 - 0.6 * math.exp(-0.3 * 0)

    w_in_b = w_in[0].astype(BF16)
    wa, wb, wo = w_proj_a[0].astype(BF16), w_proj_b[0].astype(BF16), w_out[0].astype(BF16)
    wi, wo_e = w_exp_in[0].astype(BF16), w_exp_out[0].astype(BF16)
    bi, bo = b_exp_in[0][:, None, :], b_exp_out[0][:, None, :]
    lamv = jnp.stack([df_lambda_q1[0], df_lambda_k1[0], df_lambda_q2[0], df_lambda_k2[0]]).astype(F32)
    g_sub = g_subln[0][None, :]
    g_a, g_f, g_fin = g_attn[0][None, :], g_ffn[0][None, :], g_final[None, :]
    b_r = b_router[0][None, :]

    far = rel_bias[N_BUCKETS - 1]
    t_i = jnp.arange(tq)[:, None]
    s_i = jnp.arange(tq)[None, :]
    near = jnp.stack([jnp.maximum(t_i - s_i, 0), t_i - s_i + tq])
    bias_tiles = jnp.transpose(rel_bias[_bucket(near)] - far, (3, 0, 1, 2))
    r_i = (jnp.arange(SB_HEADS * dq) % dq)[:, None]
    c_i = jnp.arange(page)[None, :]
    dec_dist = jnp.stack([jnp.maximum(r_i - c_i, 0), r_i + page - c_i])
    dec_bias = rel_bias[_bucket(dec_dist)] - far
    head_of_row = jnp.arange(SB_HEADS * dq) // (2 * dq)
    dec_bias = jnp.take_along_axis(dec_bias, head_of_row[None, :, None, None], axis=3)[..., 0]

    xp = x_prompt.reshape(batch * seq, D_MODEL)
    xs = x_sample.reshape(nb * dq, D_MODEL)

    qkv_p, sk_p, sv_p, dk_p, dv_p, gates_p = _inproj(xp, g_a, w_in_b, _tile(batch * seq, 256))
    qkv_s, sk_s, sv_s, dk_s, dv_s, gates_s = _inproj(xs, g_a, w_in_b, _tile(nb * dq, 256))

    sb_p = _sb_attn(qkv_p, batch, seq, tq)
    df_p = _df_attn(qkv_p, lamv, bias_tiles, g_sub, batch, seq, tq, lambda_init)

    caches = [c[0].reshape(n_pool, page, BRANCH_W) for c in (cache_sb_k, cache_sb_v, cache_df_k, cache_df_v)]
    sb_s, df_s = _decode(page_table, lamv, qkv_s.reshape(nb, dq, QKV_W), caches, dec_bias, g_sub, lambda_init)
    sb_s = sb_s.reshape(nb * dq, BRANCH_W)
    df_s = df_s.reshape(nb * dq, BRANCH_W)

    def tail(sb, df, gates, x):
        n = x.shape[0]
        x1, h2, comb = _post(sb, df, gates, x, b_gate[0], wa, wb, wo, g_f, w_router[0], b_r, _tile(n, 512))
        return _moe(h2, comb, x1, wi, bi, wo_e, bo, g_fin, _tile(n, 1024))

    y_p = tail(sb_p, df_p, gates_p, xp).reshape(batch, seq, D_MODEL)
    y_s = tail(sb_s, df_s, gates_s, xs).reshape(nb, dq, D_MODEL)

    def rows(a, lead, shape):
        return a.reshape((depth,) + lead + shape)

    p_lead, s_lead = (batch, seq), (nb, dq)
    return (y_p, y_s,
            rows(sk_p, p_lead, (SB_HEADS, HEAD_DIM)), rows(sv_p, p_lead, (SB_HEADS, HEAD_DIM)),
            rows(dk_p, p_lead, (DF_HEADS, 2, HEAD_DIM)), rows(dv_p, p_lead, (DF_HEADS, 2 * HEAD_DIM)),
            rows(sk_s, s_lead, (SB_HEADS, HEAD_DIM)), rows(sv_s, s_lead, (SB_HEADS, HEAD_DIM)),
            rows(dk_s, s_lead, (DF_HEADS, 2, HEAD_DIM)), rows(dv_s, s_lead, (DF_HEADS, 2 * HEAD_DIM)))
```

```python
import functools
import math

import numpy as np
import jax
import jax.numpy as jnp
from jax import lax
from jax.experimental import pallas as pl
from jax.experimental.pallas import tpu as pltpu

F32 = jnp.float32
BF16 = jnp.bfloat16

D_MODEL = 1024
SB_HEADS = 8
DF_HEADS = 4
HEAD_DIM = 64
BRANCH_W = 512
LANES = 128
N_BUCKETS = 32
MAX_DISTANCE = 128
N_EXPERTS = 32
TOP_K = 4
D_FF = 1024
SWIGLU_LIMIT = 7.0
SWIGLU_ALPHA = 1.702
RMS_EPS = 1e-5
QK_SCALE = HEAD_DIM ** -0.5
LOG2E = math.log2(math.e)

BLK = BRANCH_W // LANES
COL_SQ, COL_SK, COL_SV, COL_DK = 0, BLK, 2 * BLK, 3 * BLK
SAMPLE_QKV_W = 6 * BRANCH_W
PROMPT_QKV_W = 4 * BRANCH_W

SB_STOP = 110.0
NEG_BIG = -1e30
VMEM_LIMIT = 56 * 1024 * 1024
PAGES_PER_STEP = 4


def _nt_dot(a, b):
    return lax.dot_general(a, b, (((1,), (1,)), ((), ())), preferred_element_type=F32)


def _dot(a, b):
    return jnp.dot(a, b, preferred_element_type=F32)


def _split_dot(x, w_bf16):
    hi = x.astype(BF16)
    lo = (x - hi.astype(F32)).astype(BF16)
    return _dot(hi, w_bf16) + _dot(lo, w_bf16)


def _softplus(z):
    return jnp.maximum(z, 0.0) + jnp.log1p(jnp.exp(-jnp.abs(z)))


def _rms(x, g):
    return x * lax.rsqrt(jnp.mean(x * x, axis=-1, keepdims=True) + RMS_EPS) * g


def _bias_from_buckets(bucket, value_of):
    out = jnp.zeros(bucket.shape, F32)
    for b in range(N_BUCKETS):
        out = jnp.where(bucket == b, value_of(b), out)
    return out


def _inproj_sample_kernel(x_ref, g_ref, w_ref, qkv_ref, sk_ref, sv_ref, dk_ref, dv_ref, gate_ref):
    h = _rms(x_ref[...], g_ref[...]).astype(BF16)

    def sec(c):
        return _dot(h, w_ref[:, c * BRANCH_W:(c + 1) * BRANCH_W])

    for c, (scale, f32_out) in enumerate(((QK_SCALE, None), (1.0, sk_ref), (1.0, sv_ref),
                                          (QK_SCALE, None), (1.0, dk_ref), (1.0, dv_ref))):
        p = sec(c)
        if f32_out is not None:
            f32_out[...] = p
        qkv_ref[:, c * BRANCH_W:(c + 1) * BRANCH_W] = (p * scale).astype(BF16)
    for c in range(4):
        gate_ref[:, c * BRANCH_W:(c + 1) * BRANCH_W] = sec(6 + c)


def _inproj_sample(x, g, w_bf16, tm):
    n = x.shape[0]
    in_w = w_bf16.shape[1]
    kv = jax.ShapeDtypeStruct((n, BRANCH_W), F32)
    row = lambda i: (i, 0)
    return pl.pallas_call(
        _inproj_sample_kernel,
        grid=(n // tm,),
        in_specs=[pl.BlockSpec((tm, D_MODEL), row),
                  pl.BlockSpec((1, D_MODEL), lambda i: (0, 0)),
                  pl.BlockSpec((D_MODEL, in_w), lambda i: (0, 0))],
        out_specs=[pl.BlockSpec((tm, SAMPLE_QKV_W), row)] + [pl.BlockSpec((tm, BRANCH_W), row)] * 4
                  + [pl.BlockSpec((tm, 2 * D_MODEL), row)],
        out_shape=[jax.ShapeDtypeStruct((n, SAMPLE_QKV_W), BF16), kv, kv, kv, kv,
                   jax.ShapeDtypeStruct((n, 2 * D_MODEL), F32)],
        compiler_params=pltpu.CompilerParams(dimension_semantics=("arbitrary",),
                                             vmem_limit_bytes=VMEM_LIMIT),
        name="inproj_sample",
    )(x, g, w_bf16)


def _inproj_prompt_kernel(x_ref, g_ref, w_ref, wt_ref, qkv_ref, dv_ref, gate_ref,
                          skt_ref, svt_ref, dkt_ref, dqt_ref, dvt_ref):
    h = _rms(x_ref[...], g_ref[...]).astype(BF16)

    def sec(c):
        return _dot(h, w_ref[:, c * BRANCH_W:(c + 1) * BRANCH_W])

    def sec_t(c):
        return _nt_dot(wt_ref[c * BRANCH_W:(c + 1) * BRANCH_W, :], h)

    qkv_ref[:, 0 * BRANCH_W:1 * BRANCH_W] = (sec(0) * QK_SCALE).astype(BF16)
    for c in (1, 2, 3):
        qkv_ref[:, c * BRANCH_W:(c + 1) * BRANCH_W] = sec(c).astype(BF16)
    dv_ref[...] = sec(4)
    for c in range(4):
        gate_ref[:, c * BRANCH_W:(c + 1) * BRANCH_W] = sec(5 + c)
    skt_ref[0] = sec_t(0)
    svt_ref[0] = sec_t(1)
    dkt_ref[0] = sec_t(2)
    dqt_ref[...] = (sec_t(3) * (QK_SCALE * LOG2E)).astype(BF16)
    dvt_ref[0] = sec_t(4).astype(BF16)


def _inproj_prompt(x, g, w_bf16, wt_bf16, batch, seq, tm):
    n = x.shape[0]
    per_seq = seq // tm
    row = lambda i: (i, 0)
    const = lambda i: (0, 0)
    kvt = jax.ShapeDtypeStruct((batch, BRANCH_W, seq), F32)
    kvt_spec = pl.BlockSpec((1, BRANCH_W, tm), lambda i: (i // per_seq, 0, i % per_seq))
    return pl.pallas_call(
        _inproj_prompt_kernel,
        grid=(n // tm,),
        in_specs=[pl.BlockSpec((tm, D_MODEL), row), pl.BlockSpec((1, D_MODEL), const),
                  pl.BlockSpec(w_bf16.shape, const), pl.BlockSpec(wt_bf16.shape, const)],
        out_specs=[pl.BlockSpec((tm, PROMPT_QKV_W), row), pl.BlockSpec((tm, BRANCH_W), row),
                   pl.BlockSpec((tm, 2 * D_MODEL), row), kvt_spec, kvt_spec, kvt_spec,
                   pl.BlockSpec((BRANCH_W, tm), lambda i: (0, i)),
                   pl.BlockSpec((1, BRANCH_W, tm), lambda i: (i, 0, 0))],
        out_shape=[jax.ShapeDtypeStruct((n, PROMPT_QKV_W), BF16), jax.ShapeDtypeStruct((n, BRANCH_W), F32),
                   jax.ShapeDtypeStruct((n, 2 * D_MODEL), F32), kvt, kvt, kvt,
                   jax.ShapeDtypeStruct((BRANCH_W, n), BF16),
                   jax.ShapeDtypeStruct((n // tm, BRANCH_W, tm), BF16)],
        compiler_params=pltpu.CompilerParams(dimension_semantics=("arbitrary",),
                                             vmem_limit_bytes=VMEM_LIMIT),
        name="inproj_prompt",
    )(x, g, w_bf16, wt_bf16)


def _half_masks():
    lane = lax.broadcasted_iota(jnp.int32, (1, LANES), 1)
    return lane < HEAD_DIM, lane >= HEAD_DIM


def _stack_halves(q):
    lo, hi = _half_masks()
    zero = jnp.zeros_like(q)
    return jnp.concatenate([jnp.where(lo, q, zero), jnp.where(hi, q, zero)], axis=0)


def _strict_upper_ones(n):
    j = lax.broadcasted_iota(jnp.int32, (n, n), 0)
    s = lax.broadcasted_iota(jnp.int32, (n, n), 1)
    return jnp.where(j > s, 1.0, 0.0).astype(BF16)


def _sb_scores(z, mask, tri):
    sp = _softplus(z)
    log_keep = -sp if mask is None else jnp.where(mask, -sp, 0.0)
    return log_keep, z - sp + _split_dot(log_keep, tri)


def _sb_attn_kernel(q_ref, k_ref, v_ref, o_ref, acc_ref, car_ref, *, tq):
    i = pl.program_id(2)
    q2 = _stack_halves(q_ref[...])
    tri = _strict_upper_ones(tq)
    row = lax.broadcasted_iota(jnp.int32, (2 * tq, tq), 0)
    row = jnp.where(row >= tq, row - tq, row)
    col = lax.broadcasted_iota(jnp.int32, (2 * tq, tq), 1)
    acc_ref[...] = jnp.zeros_like(acc_ref)
    car_ref[...] = jnp.zeros_like(car_ref)

    def body(state):
        j, _ = state
        ks = pl.multiple_of(j * tq, tq)
        mask = (col + (j - i) * tq) < row
        log_keep, logit = _sb_scores(_nt_dot(q2, k_ref[pl.ds(ks, tq), :]), mask, tri)
        carry = car_ref[...]
        a = jnp.where(mask, jnp.exp(logit + carry), 0.0)
        acc_ref[...] += _dot(a.astype(BF16), v_ref[pl.ds(ks, tq), :])
        carry = carry + jnp.sum(log_keep, axis=1, keepdims=True)
        car_ref[...] = carry
        return j - 1, jnp.max(carry)

    lax.while_loop(lambda s: (s[0] >= 0) & (s[1] > -SB_STOP), body, (i, jnp.float32(0.0)))
    lo, _ = _half_masks()
    o_ref[...] = jnp.where(lo, acc_ref[:tq, :], acc_ref[tq:, :]).astype(o_ref.dtype)


def _sb_attn(qkv, batch, seq, tq):
    nq = seq // tq
    return pl.pallas_call(
        functools.partial(_sb_attn_kernel, tq=tq),
        grid=(batch, BLK, nq),
        in_specs=[pl.BlockSpec((tq, LANES), lambda b, p, i: (b * nq + i, COL_SQ + p)),
                  pl.BlockSpec((seq, LANES), lambda b, p, i: (b, COL_SK + p)),
                  pl.BlockSpec((seq, LANES), lambda b, p, i: (b, COL_SV + p))],
        out_specs=pl.BlockSpec((tq, LANES), lambda b, p, i: (b * nq + i, p)),
        out_shape=jax.ShapeDtypeStruct((batch * seq, BRANCH_W), BF16),
        scratch_shapes=[pltpu.VMEM((2 * tq, LANES), F32), pltpu.VMEM((2 * tq, 1), F32)],
        compiler_params=pltpu.CompilerParams(dimension_semantics=("arbitrary",) * 3,
                                             vmem_limit_bytes=VMEM_LIMIT),
        name="sb_attn",
    )(qkv, qkv, qkv)


def _lambda(lamv_ref, lambda_init):
    a = jnp.sum(lamv_ref[0:1, :] * lamv_ref[1:2, :], axis=1, keepdims=True)
    b = jnp.sum(lamv_ref[2:3, :] * lamv_ref[3:4, :], axis=1, keepdims=True)
    return jnp.exp(a) - jnp.exp(b) + lambda_init


def _df_attn_kernel(tab_ref, lamv_ref, qt_ref, k_ref, vt_ref, bkt_ref, g_ref, o_ref,
                    bias_s, m_s, l_s, acc_s, *, tq, lambda_init):
    h = pl.program_id(1)
    i = pl.program_id(2)

    @pl.when(i == 0)
    def _():
        far = tab_ref[N_BUCKETS - 1, h]
        for t in range(2):
            bias_s[t] = _bias_from_buckets(bkt_ref[t], lambda b: (tab_ref[b, h] - far) * LOG2E)

    qt = qt_ref[...].astype(F32)
    feat = lax.broadcasted_iota(jnp.int32, qt.shape, 0)
    q2t = jnp.concatenate([jnp.where(feat < HEAD_DIM, qt, 0.0), jnp.where(feat >= HEAD_DIM, qt, 0.0)],
                          axis=1).astype(BF16)
    m_s[...] = jnp.full_like(m_s, NEG_BIG)
    l_s[...] = jnp.zeros_like(l_s)
    acc_s[...] = jnp.zeros_like(acc_s)

    def step(blocks):
        scores = []
        for j, bias, mask in blocks:
            s = _dot(k_ref[pl.ds(pl.multiple_of(j * tq, tq), tq), :], q2t)
            if bias is not None:
                s = s + jnp.concatenate([bias, bias], axis=1)
            if mask is not None:
                s = jnp.where(mask, s, NEG_BIG)
            scores.append(s)
        m_old = m_s[...]
        m_new = m_old
        for s in scores:
            m_new = jnp.maximum(m_new, jnp.max(s, axis=0, keepdims=True))
        alpha = jnp.exp2(m_old - m_new)
        l_new = alpha * l_s[...]
        acc = alpha * acc_s[...]
        for (j, _, _), s in zip(blocks, scores):
            p = jnp.exp2(s - m_new)
            l_new = l_new + jnp.sum(p, axis=0, keepdims=True)
            acc = acc + _dot(vt_ref[j], p.astype(BF16))
        l_s[...] = l_new
        acc_s[...] = acc
        m_s[...] = m_new

    n_far = jnp.maximum(i - 1, 0)

    def far_pair(jj, c):
        step([(2 * jj, None, None), (2 * jj + 1, None, None)])
        return c

    lax.fori_loop(0, n_far // 2, far_pair, 0)

    @pl.when(n_far % 2 == 1)
    def _():
        step([(n_far - 1, None, None)])

    key = lax.broadcasted_iota(jnp.int32, (tq, 2 * tq), 0)
    qry = lax.broadcasted_iota(jnp.int32, (tq, 2 * tq), 1)
    qry = jnp.where(qry >= tq, qry - tq, qry)
    diag = (i, bias_s[0], key <= qry)

    @pl.when(i >= 1)
    def _():
        step([(i - 1, bias_s[1], None), diag])

    @pl.when(i == 0)
    def _():
        step([diag])

    lam = _lambda(lamv_ref, lambda_init)
    norm = acc_s[...] / l_s[...]
    out = (norm[:, :tq] - lam * norm[:, tq:]).T
    o_ref[...] = (_rms(out, g_ref[...]) * (1.0 - lambda_init)).astype(o_ref.dtype)


def _df_attn(tab, lamv, dqt, qkv, dvt, buckets, g_subln, batch, seq, tq, lambda_init):
    nq = seq // tq
    return pl.pallas_call(
        functools.partial(_df_attn_kernel, tq=tq, lambda_init=lambda_init),
        grid=(batch, DF_HEADS, nq),
        in_specs=[pl.BlockSpec(memory_space=pltpu.SMEM),
                  pl.BlockSpec((4, HEAD_DIM), lambda b, h, i: (0, 0)),
                  pl.BlockSpec((LANES, tq), lambda b, h, i: (h, b * nq + i)),
                  pl.BlockSpec((seq, LANES), lambda b, h, i: (b, COL_DK + h)),
                  pl.BlockSpec((nq, LANES, tq), lambda b, h, i: (b, h, 0)),
                  pl.BlockSpec((2, tq, tq), lambda b, h, i: (0, 0, 0)),
                  pl.BlockSpec((1, LANES), lambda b, h, i: (0, 0))],
        out_specs=pl.BlockSpec((tq, LANES), lambda b, h, i: (b * nq + i, h)),
        out_shape=jax.ShapeDtypeStruct((batch * seq, BRANCH_W), BF16),
        scratch_shapes=[pltpu.VMEM((2, tq, tq), F32), pltpu.VMEM((1, 2 * tq), F32),
                        pltpu.VMEM((1, 2 * tq), F32), pltpu.VMEM((LANES, 2 * tq), F32)],
        compiler_params=pltpu.CompilerParams(dimension_semantics=("arbitrary",) * 3,
                                             vmem_limit_bytes=VMEM_LIMIT),
        name="df_attn",
    )(tab, lamv, dqt, qkv, dvt, buckets, g_subln)


def _decode_kernel(pt_ref, tab_ref, lamv_ref, qkv_ref, *rest, dq, page, lambda_init, group):
    ksb, vsb, kdf, vdf = (rest[g * group:(g + 1) * group] for g in range(4))
    bkt_ref, g_ref, osb_ref, odf_ref, qsb_s, qdf_s, bias_s, acc_sb, car_s, acc_df, m_s, l_s = rest[4 * group:]
    s = pl.program_id(1)
    n_steps = pl.num_programs(1)
    rows = SB_HEADS * dq
    hrows = rows // DF_HEADS
    tri = _strict_upper_ones(page)

    def sb_update(z_list, mask, v_list, v_transposed):
        carry = car_s[...]
        acc = acc_sb[...]
        for z, v in zip(z_list, v_list):
            log_keep, logit = _sb_scores(z, mask, tri)
            a = jnp.exp(logit + carry)
            if mask is not None:
                a = jnp.where(mask, a, 0.0)
            a = a.astype(BF16)
            acc = acc + (_nt_dot(a, v) if v_transposed else _dot(a, v))
            carry = carry + jnp.sum(log_keep, axis=1, keepdims=True)
        acc_sb[...] = acc
        car_s[...] = carry

    def df_update(s_list, v_of):
        m_old = m_s[...]
        m_new = m_old
        for sc in s_list:
            m_new = jnp.maximum(m_new, jnp.max(sc, axis=1, keepdims=True))
        alpha = jnp.exp(m_old - m_new)
        p_list = [jnp.exp(sc - m_new) for sc in s_list]
        l_new = alpha * l_s[...]
        for p in p_list:
            l_new = l_new + jnp.sum(p, axis=1, keepdims=True)
        for h in range(DF_HEADS):
            sl = slice(h * hrows, (h + 1) * hrows)
            acc = alpha[sl] * acc_df[sl, :]
            for g, p in enumerate(p_list):
                acc = acc + _dot(p[sl].astype(BF16), v_of(g, h))
            acc_df[sl, :] = acc
        m_s[...] = m_new
        l_s[...] = l_new

    @pl.when(s == 0)
    def _():
        rq = lax.broadcasted_iota(jnp.int32, (rows, BRANCH_W), 0) // dq
        lq = lax.broadcasted_iota(jnp.int32, (rows, BRANCH_W), 1) // HEAD_DIM
        for c, dst in ((0, qsb_s), (3, qdf_s)):
            q = qkv_ref[0, :, c * BRANCH_W:(c + 1) * BRANCH_W].astype(F32)
            qt = jnp.concatenate([q] * SB_HEADS, axis=0)
            dst[...] = jnp.where(rq == lq, qt, 0.0).astype(BF16)
        for h in range(DF_HEADS):
            far = tab_ref[N_BUCKETS - 1, h]
            for t in range(2):
                bias_s[t, h * hrows:(h + 1) * hrows, :] = _bias_from_buckets(
                    bkt_ref[t, h * hrows:(h + 1) * hrows, :], lambda b: tab_ref[b, h] - far)
        acc_sb[...] = jnp.zeros_like(acc_sb)
        car_s[...] = jnp.zeros_like(car_s)
        acc_df[...] = jnp.zeros_like(acc_df)
        m_s[...] = jnp.full_like(m_s, NEG_BIG)
        l_s[...] = jnp.zeros_like(l_s)

        def new_rows(c):
            x = qkv_ref[0, :, c * BRANCH_W:(c + 1) * BRANCH_W].astype(F32)
            return jnp.concatenate([x, jnp.zeros((page - dq, BRANCH_W), F32)], axis=0).astype(BF16)

        row = lax.broadcasted_iota(jnp.int32, (rows, page), 0) % dq
        col = lax.broadcasted_iota(jnp.int32, (rows, page), 1)
        sb_update([_nt_dot(qsb_s[...], new_rows(1))], col < row, [new_rows(2)], False)
        v_new = new_rows(5)
        sc = jnp.where(col <= row, _nt_dot(qdf_s[...], new_rows(4)) + bias_s[0], NEG_BIG)
        df_update([sc], lambda g, h: v_new[:, h * LANES:(h + 1) * LANES])

    def cached_step(newest):
        @pl.when(jnp.max(car_s[...]) > -SB_STOP)
        def _():
            q = qsb_s[...]
            sb_update([_dot(q, k[...].astype(BF16)) for k in ksb], None,
                      [v[...].astype(BF16) for v in vsb], True)

        q = qdf_s[...]
        scores = [_dot(q, k[...].astype(BF16)) for k in kdf]
        if newest:
            scores[0] = scores[0] + bias_s[1]
        df_update(scores, lambda g, h: vdf[g][pl.ds(h, page, stride=DF_HEADS), :].astype(BF16))

    @pl.when(s == 1)
    def _():
        cached_step(True)

    @pl.when(s >= 2)
    def _():
        cached_step(False)

    @pl.when(s == n_steps - 1)
    def _():
        rq = lax.broadcasted_iota(jnp.int32, (rows, BRANCH_W), 0)
        lq = lax.broadcasted_iota(jnp.int32, (rows, BRANCH_W), 1)
        sb = jnp.where(rq // dq == lq // HEAD_DIM, acc_sb[...], 0.0)
        sb = functools.reduce(lambda a, b: a + b, [sb[g * dq:(g + 1) * dq, :] for g in range(SB_HEADS)])
        osb_ref[0] = sb.astype(osb_ref.dtype)

        lam = _lambda(lamv_ref, lambda_init)
        norm = acc_df[...] / l_s[...]
        g = g_ref[...]
        heads = []
        for h in range(DF_HEADS):
            o = norm[h * hrows:h * hrows + dq, :] - lam * norm[h * hrows + dq:(h + 1) * hrows, :]
            heads.append(_rms(o, g))
        odf_ref[0] = (jnp.concatenate(heads, axis=1) * (1.0 - lambda_init)).astype(odf_ref.dtype)


def _decode(page_table, tab, lamv, qkv3, caches, dec_buckets, g_subln, lambda_init, page):
    nb, dq, _ = qkv3.shape
    n_pages = page_table.shape[1]
    group = PAGES_PER_STEP
    while n_pages % group:
        group //= 2
    rows = SB_HEADS * dq

    def cache_spec(g):
        def index(b, s, pt):
            return (pt[b, n_pages - 1 - (jnp.maximum(s - 1, 0) * group + g)], 0)
        return pl.BlockSpec((BRANCH_W, page), index)

    cache_specs = [cache_spec(g) for _ in range(4) for g in range(group)]
    cache_args = [c for c in caches for _ in range(group)]
    out = jax.ShapeDtypeStruct((nb, dq, BRANCH_W), BF16)
    const2 = lambda b, s, pt: (0, 0)
    grid_spec = pltpu.PrefetchScalarGridSpec(
        num_scalar_prefetch=1,
        grid=(nb, n_pages // group + 1),
        in_specs=[pl.BlockSpec(memory_space=pltpu.SMEM),
                  pl.BlockSpec((4, HEAD_DIM), const2),
                  pl.BlockSpec((1, dq, SAMPLE_QKV_W), lambda b, s, pt: (b, 0, 0))]
                 + cache_specs
                 + [pl.BlockSpec((2, rows, page), lambda b, s, pt: (0, 0, 0)),
                    pl.BlockSpec((1, LANES), const2)],
        out_specs=[pl.BlockSpec((1, dq, BRANCH_W), lambda b, s, pt: (b, 0, 0))] * 2,
        scratch_shapes=[pltpu.VMEM((rows, BRANCH_W), BF16), pltpu.VMEM((rows, BRANCH_W), BF16),
                        pltpu.VMEM((2, rows, page), F32),
                        pltpu.VMEM((rows, BRANCH_W), F32), pltpu.VMEM((rows, 1), F32),
                        pltpu.VMEM((rows, LANES), F32), pltpu.VMEM((rows, 1), F32),
                        pltpu.VMEM((rows, 1), F32)],
    )
    return pl.pallas_call(
        functools.partial(_decode_kernel, dq=dq, page=page, lambda_init=lambda_init, group=group),
        grid_spec=grid_spec,
        out_shape=[out, out],
        compiler_params=pltpu.CompilerParams(dimension_semantics=("arbitrary", "arbitrary"),
                                             vmem_limit_bytes=VMEM_LIMIT),
        name="decode_attn",
    )(page_table, tab, lamv, qkv3, *cache_args, dec_buckets, g_subln)


def _post_kernel(sb_ref, df_ref, ga_ref, gb_ref, x_ref, bg_ref, wa_ref, wb_ref, wo_ref, gf_ref,
                 wr_ref, br_ref, x1_ref, h2_ref, comb_ref):
    a = _dot(sb_ref[...], wa_ref[...])
    b = _dot(df_ref[...], wb_ref[...])
    merged = jax.nn.sigmoid(ga_ref[...] + bg_ref[0:1, :]) * a + jax.nn.sigmoid(gb_ref[...] + bg_ref[1:2, :]) * b
    x1 = x_ref[...] + _dot(merged.astype(BF16), wo_ref[...])
    x1_ref[...] = x1
    h2 = _rms(x1, gf_ref[...])
    h2_ref[...] = h2.astype(BF16)

    wr = wr_ref[...]
    wr_hi = wr.astype(BF16)
    wr_lo = (wr - wr_hi.astype(F32)).astype(BF16)
    h_hi = h2.astype(BF16)
    h_lo = (h2 - h_hi.astype(F32)).astype(BF16)
    logits = _dot(h_hi, wr_hi) + _dot(h_hi, wr_lo) + _dot(h_lo, wr_hi) + br_ref[...]

    lane = lax.broadcasted_iota(jnp.int32, logits.shape, 1).astype(F32)
    vals, hots = [], []
    for _ in range(TOP_K):
        mx = jnp.max(logits, axis=1, keepdims=True)
        first = jnp.min(jnp.where(logits == mx, lane, float(N_EXPERTS)), axis=1, keepdims=True)
        hot = lane == first
        vals.append(mx)
        hots.append(hot)
        logits = jnp.where(hot, -jnp.inf, logits)
    es = [jnp.exp(v - vals[0]) for v in vals]
    denom = functools.reduce(lambda u, v: u + v, es)
    comb = jnp.zeros(logits.shape, F32)
    for e, hot in zip(es, hots):
        comb = comb + jnp.where(hot, e / denom, 0.0)
    comb_ref[...] = comb


def _post(sb, df, gates, x, b_gate, wa, wb, wo, g_ffn, w_router, b_router, tm):
    n = x.shape[0]
    row = lambda i: (i, 0)
    const = lambda i: (0, 0)
    return pl.pallas_call(
        _post_kernel,
        grid=(n // tm,),
        in_specs=[pl.BlockSpec((tm, BRANCH_W), row), pl.BlockSpec((tm, BRANCH_W), row),
                  pl.BlockSpec((tm, D_MODEL), lambda i: (i, 0)), pl.BlockSpec((tm, D_MODEL), lambda i: (i, 1)),
                  pl.BlockSpec((tm, D_MODEL), row),
                  pl.BlockSpec((2, D_MODEL), const),
                  pl.BlockSpec((BRANCH_W, D_MODEL), const), pl.BlockSpec((BRANCH_W, D_MODEL), const),
                  pl.BlockSpec((D_MODEL, D_MODEL), const),
                  pl.BlockSpec((1, D_MODEL), const),
                  pl.BlockSpec((D_MODEL, N_EXPERTS), const), pl.BlockSpec((1, N_EXPERTS), const)],
        out_specs=[pl.BlockSpec((tm, D_MODEL), row), pl.BlockSpec((tm, D_MODEL), row),
                   pl.BlockSpec((tm, N_EXPERTS), row)],
        out_shape=[jax.ShapeDtypeStruct((n, D_MODEL), F32), jax.ShapeDtypeStruct((n, D_MODEL), BF16),
                   jax.ShapeDtypeStruct((n, N_EXPERTS), F32)],
        compiler_params=pltpu.CompilerParams(dimension_semantics=("arbitrary",),
                                             vmem_limit_bytes=VMEM_LIMIT),
        name="post_attn",
    )(sb, df, gates, gates, x, b_gate, wa, wb, wo, g_ffn, w_router, b_router)


def _moe_kernel(h_ref, comb_ref, x1_ref, wi_ref, bi_ref, wo_ref, bo_ref, gfin_ref, y_ref, acc_ref):
    e = pl.program_id(1)

    @pl.when(e == 0)
    def _():
        acc_ref[...] = jnp.zeros_like(acc_ref)

    h = h_ref[...]
    gate = jnp.minimum(_dot(h, wi_ref[0, :, :D_FF]) + bi_ref[0, :, :D_FF], SWIGLU_LIMIT)
    up = jnp.clip(_dot(h, wi_ref[0, :, D_FF:]) + bi_ref[0, :, D_FF:], -SWIGLU_LIMIT, SWIGLU_LIMIT)
    act = (up + 1.0) * gate * jax.nn.sigmoid(SWIGLU_ALPHA * gate)
    out = _dot(act.astype(BF16), wo_ref[0]) + bo_ref[0]
    comb = comb_ref[...]
    lane = lax.broadcasted_iota(jnp.int32, comb.shape, 1)
    w = jnp.sum(jnp.where(lane == e, comb, 0.0), axis=1, keepdims=True)
    acc_ref[...] += w * out

    @pl.when(e == pl.num_programs(1) - 1)
    def _():
        y_ref[...] = _rms(x1_ref[...] + acc_ref[...], gfin_ref[...])


def _moe(h2, comb, x1, wi, bi, wo, bo, g_final, tm):
    n = h2.shape[0]
    row = lambda i, e: (i, 0)
    exp3 = lambda i, e: (e, 0, 0)
    return pl.pallas_call(
        _moe_kernel,
        grid=(n // tm, N_EXPERTS),
        in_specs=[pl.BlockSpec((tm, D_MODEL), row), pl.BlockSpec((tm, N_EXPERTS), row),
                  pl.BlockSpec((tm, D_MODEL), row),
                  pl.BlockSpec((1, D_MODEL, 2 * D_FF), exp3), pl.BlockSpec((1, 1, 2 * D_FF), exp3),
                  pl.BlockSpec((1, D_FF, D_MODEL), exp3), pl.BlockSpec((1, 1, D_MODEL), exp3),
                  pl.BlockSpec((1, D_MODEL), lambda i, e: (0, 0))],
        out_specs=pl.BlockSpec((tm, D_MODEL), row),
        out_shape=jax.ShapeDtypeStruct((n, D_MODEL), F32),
        scratch_shapes=[pltpu.VMEM((tm, D_MODEL), F32)],
        compiler_params=pltpu.CompilerParams(dimension_semantics=("arbitrary", "arbitrary"),
                                             vmem_limit_bytes=VMEM_LIMIT),
        name="moe",
    )(h2, comb, x1, wi, bi, wo, bo, g_final)


def _bucket_np(n):
    max_exact = N_BUCKETS // 2
    nf = np.maximum(n, max_exact).astype(np.float32)
    large = max_exact + (np.log(nf / np.float32(max_exact)) / np.float32(math.log(MAX_DISTANCE / max_exact))
                         * np.float32(N_BUCKETS - max_exact)).astype(np.int32)
    return np.where(n < max_exact, n, np.minimum(large, N_BUCKETS - 1)).astype(np.int32)


def _tile(n, cap):
    t = cap
    while n % t:
        t //= 2
    return t


def kernel(x_prompt, x_sample, cache_sb_k, cache_sb_v, cache_df_k, cache_df_v, page_table, rel_bias, g_attn, w_in, b_gate, df_lambda_q1, df_lambda_k1, df_lambda_q2, df_lambda_k2, g_subln, w_proj_a, w_proj_b, w_out, g_ffn, w_router, b_router, w_exp_in, b_exp_in, w_exp_out, b_exp_out, g_final):
    depth = w_in.shape[0]
    assert depth == 1, "single-layer step"
    batch, seq, _ = x_prompt.shape
    nb, dq, _ = x_sample.shape
    n_pool, page = cache_sb_k.shape[1], cache_sb_k.shape[2]
    tq = _tile(seq, 256)
    assert tq >= MAX_DISTANCE and page >= MAX_DISTANCE, "bias is constant beyond the two nearest tiles"
    lambda_init = 0.8 - 0.6 * math.exp(-0.3 * 0)

    w = w_in[0]
    sec = lambda c: w[:, c * BRANCH_W:(c + 1) * BRANCH_W]
    w_in_b = w.astype(BF16)
    w_rows = jnp.concatenate([sec(0), sec(1), sec(2), sec(4), sec(5), w[:, 6 * BRANCH_W:]], axis=1).astype(BF16)
    w_cols = jnp.concatenate([sec(1), sec(2), sec(4), sec(3), sec(5)], axis=1).T.astype(BF16)
    wa, wb, wo = w_proj_a[0].astype(BF16), w_proj_b[0].astype(BF16), w_out[0].astype(BF16)
    wi, wo_e = w_exp_in[0].astype(BF16), w_exp_out[0].astype(BF16)
    bi, bo = b_exp_in[0][:, None, :], b_exp_out[0][:, None, :]
    lamv = jnp.stack([df_lambda_q1[0], df_lambda_k1[0], df_lambda_q2[0], df_lambda_k2[0]]).astype(F32)
    g_sub = g_subln[0][None, :]
    g_a, g_f, g_fin = g_attn[0][None, :], g_ffn[0][None, :], g_final[None, :]
    b_r = b_router[0][None, :]

    k_i = np.arange(tq)[:, None]
    q_i = np.arange(tq)[None, :]
    near_buckets = jnp.asarray(_bucket_np(np.stack([np.maximum(q_i - k_i, 0), q_i - k_i + tq])))
    r_i = (np.arange(SB_HEADS * dq) % dq)[:, None]
    c_i = np.arange(page)[None, :]
    dec_buckets = jnp.asarray(_bucket_np(np.stack([np.maximum(r_i - c_i, 0), r_i + page - c_i])))

    xp = x_prompt.reshape(batch * seq, D_MODEL)
    xs = x_sample.reshape(nb * dq, D_MODEL)

    qkv_p, dv_p, gates_p, skt, svt, dkt, dqt, dvt = _inproj_prompt(xp, g_a, w_rows, w_cols, batch, seq, tq)
    qkv_s, sk_s, sv_s, dk_s, dv_s, gates_s = _inproj_sample(xs, g_a, w_in_b, _tile(nb * dq, 256))

    sb_p = _sb_attn(qkv_p, batch, seq, tq)
    df_p = _df_attn(rel_bias, lamv, dqt, qkv_p, dvt, near_buckets, g_sub, batch, seq, tq, lambda_init)

    rows_of_pool = n_pool * BRANCH_W
    caches = [jnp.transpose(cache_sb_k[0], (0, 2, 3, 1)).reshape(rows_of_pool, page),
              jnp.transpose(cache_sb_v[0], (0, 2, 3, 1)).reshape(rows_of_pool, page),
              jnp.transpose(cache_df_k[0], (0, 2, 3, 4, 1)).reshape(rows_of_pool, page),
              cache_df_v[0].reshape(rows_of_pool, LANES)]
    sb_s, df_s = _decode(page_table, rel_bias, lamv, qkv_s.reshape(nb, dq, SAMPLE_QKV_W), caches, dec_buckets,
                         g_sub, lambda_init, page)
    sb_s = sb_s.reshape(nb * dq, BRANCH_W)
    df_s = df_s.reshape(nb * dq, BRANCH_W)

    def tail(sb, df, gates, x):
        n = x.shape[0]
        x1, h2, comb = _post(sb, df, gates, x, b_gate[0], wa, wb, wo, g_f, w_router[0], b_r, _tile(n, 512))
        return _moe(h2, comb, x1, wi, bi, wo_e, bo, g_fin, _tile(n, 512))

    y_p = tail(sb_p, df_p, gates_p, xp).reshape(batch, seq, D_MODEL)
    y_s = tail(sb_s, df_s, gates_s, xs).reshape(nb, dq, D_MODEL)

    def from_t(a, feat_shape):
        a = a.reshape((batch,) + feat_shape + (seq,))
        return jnp.moveaxis(a, -1, 1)[None]

    s_lead = (depth, nb, dq)
    return (y_p, y_s,
            from_t(skt, (SB_HEADS, HEAD_DIM)), from_t(svt, (SB_HEADS, HEAD_DIM)),
            from_t(dkt, (DF_HEADS, 2, HEAD_DIM)), dv_p.reshape(depth, batch, seq, DF_HEADS, 2 * HEAD_DIM),
            sk_s.reshape(s_lead + (SB_HEADS, HEAD_DIM)), sv_s.reshape(s_lead + (SB_HEADS, HEAD_DIM)),
            dk_s.reshape(s_lead + (DF_HEADS, 2, HEAD_DIM)), dv_s.reshape(s_lead + (DF_HEADS, 2 * HEAD_DIM)))
```

```python
import functools
import math

import numpy as np
import jax
import jax.numpy as jnp
from jax import lax
from jax.experimental import pallas as pl
from jax.experimental.pallas import tpu as pltpu

F32 = jnp.float32
BF16 = jnp.bfloat16

D_MODEL = 1024
SB_HEADS = 8
DF_HEADS = 4
HEAD_DIM = 64
BRANCH_W = 512
LANES = 128
N_BUCKETS = 32
MAX_DISTANCE = 128
N_EXPERTS = 32
TOP_K = 4
D_FF = 1024
SWIGLU_LIMIT = 7.0
SWIGLU_ALPHA = 1.702
RMS_EPS = 1e-5
QK_SCALE = HEAD_DIM ** -0.5
LOG2E = math.log2(math.e)

BLK = BRANCH_W // LANES
COL_SQ, COL_SK, COL_SV, COL_DK = 0, BLK, 2 * BLK, 3 * BLK
SAMPLE_QKV_W = 6 * BRANCH_W
PROMPT_QKV_W = 4 * BRANCH_W

SB_STOP = 110.0
NEG_BIG = -1e30
VMEM_LIMIT = 56 * 1024 * 1024
PAGES_PER_STEP = 4


def _nt_dot(a, b):
    return lax.dot_general(a, b, (((1,), (1,)), ((), ())), preferred_element_type=F32)


def _dot(a, b):
    return jnp.dot(a, b, preferred_element_type=F32)


def _split_dot(x, w_bf16):
    hi = x.astype(BF16)
    lo = (x - hi.astype(F32)).astype(BF16)
    return _dot(hi, w_bf16) + _dot(lo, w_bf16)


def _softplus(z):
    return jnp.maximum(z, 0.0) + jnp.log1p(jnp.exp(-jnp.abs(z)))


def _rms(x, g):
    return x * lax.rsqrt(jnp.mean(x * x, axis=-1, keepdims=True) + RMS_EPS) * g


def _bias_from_buckets(bucket, value_of):
    out = jnp.zeros(bucket.shape, F32)
    for b in range(N_BUCKETS):
        out = jnp.where(bucket == b, value_of(b), out)
    return out


def _inproj_sample_kernel(x_ref, g_ref, w_ref, qkv_ref, sk_ref, sv_ref, dk_ref, dv_ref, gate_ref):
    h = _rms(x_ref[...], g_ref[...]).astype(BF16)

    def sec(c):
        return _dot(h, w_ref[:, c * BRANCH_W:(c + 1) * BRANCH_W])

    for c, (scale, f32_out) in enumerate(((QK_SCALE, None), (1.0, sk_ref), (1.0, sv_ref),
                                          (QK_SCALE, None), (1.0, dk_ref), (1.0, dv_ref))):
        p = sec(c)
        if f32_out is not None:
            f32_out[...] = p
        qkv_ref[:, c * BRANCH_W:(c + 1) * BRANCH_W] = (p * scale).astype(BF16)
    for c in range(4):
        gate_ref[:, c * BRANCH_W:(c + 1) * BRANCH_W] = sec(6 + c)


def _inproj_sample(x, g, w_bf16, tm):
    n = x.shape[0]
    in_w = w_bf16.shape[1]
    kv = jax.ShapeDtypeStruct((n, BRANCH_W), F32)
    row = lambda i: (i, 0)
    return pl.pallas_call(
        _inproj_sample_kernel,
        grid=(n // tm,),
        in_specs=[pl.BlockSpec((tm, D_MODEL), row),
                  pl.BlockSpec((1, D_MODEL), lambda i: (0, 0)),
                  pl.BlockSpec((D_MODEL, in_w), lambda i: (0, 0))],
        out_specs=[pl.BlockSpec((tm, SAMPLE_QKV_W), row)] + [pl.BlockSpec((tm, BRANCH_W), row)] * 4
                  + [pl.BlockSpec((tm, 2 * D_MODEL), row)],
        out_shape=[jax.ShapeDtypeStruct((n, SAMPLE_QKV_W), BF16), kv, kv, kv, kv,
                   jax.ShapeDtypeStruct((n, 2 * D_MODEL), F32)],
        compiler_params=pltpu.CompilerParams(dimension_semantics=("arbitrary",),
                                             vmem_limit_bytes=VMEM_LIMIT),
        name="inproj_sample",
    )(x, g, w_bf16)


def _inproj_prompt_kernel(x_ref, g_ref, w_ref, wt_ref, qkv_ref, dv_ref, gate_ref,
                          skt_ref, svt_ref, dkt_ref, dqt_ref, dvt_ref):
    h = _rms(x_ref[...], g_ref[...]).astype(BF16)

    def sec(c):
        return _dot(h, w_ref[:, c * BRANCH_W:(c + 1) * BRANCH_W])

    def sec_t(c):
        return _nt_dot(wt_ref[c * BRANCH_W:(c + 1) * BRANCH_W, :], h)

    qkv_ref[:, 0 * BRANCH_W:1 * BRANCH_W] = (sec(0) * QK_SCALE).astype(BF16)
    for c in (1, 2, 3):
        qkv_ref[:, c * BRANCH_W:(c + 1) * BRANCH_W] = sec(c).astype(BF16)
    dv_ref[...] = sec(4)
    for c in range(4):
        gate_ref[:, c * BRANCH_W:(c + 1) * BRANCH_W] = sec(5 + c)
    skt_ref[0] = sec_t(0)
    svt_ref[0] = sec_t(1)
    dkt_ref[0] = sec_t(2)
    dqt_ref[...] = (sec_t(3) * (QK_SCALE * LOG2E)).astype(BF16)
    dvt_ref[0] = sec_t(4).astype(BF16)


def _inproj_prompt(x, g, w_bf16, wt_bf16, batch, seq, tm):
    n = x.shape[0]
    per_seq = seq // tm
    row = lambda i: (i, 0)
    const = lambda i: (0, 0)
    kvt = jax.ShapeDtypeStruct((batch, BRANCH_W, seq), F32)
    kvt_spec = pl.BlockSpec((1, BRANCH_W, tm), lambda i: (i // per_seq, 0, i % per_seq))
    return pl.pallas_call(
        _inproj_prompt_kernel,
        grid=(n // tm,),
        in_specs=[pl.BlockSpec((tm, D_MODEL), row), pl.BlockSpec((1, D_MODEL), const),
                  pl.BlockSpec(w_bf16.shape, const), pl.BlockSpec(wt_bf16.shape, const)],
        out_specs=[pl.BlockSpec((tm, PROMPT_QKV_W), row), pl.BlockSpec((tm, BRANCH_W), row),
                   pl.BlockSpec((tm, 2 * D_MODEL), row), kvt_spec, kvt_spec, kvt_spec,
                   pl.BlockSpec((BRANCH_W, tm), lambda i: (0, i)),
                   pl.BlockSpec((1, BRANCH_W, tm), lambda i: (i, 0, 0))],
        out_shape=[jax.ShapeDtypeStruct((n, PROMPT_QKV_W), BF16), jax.ShapeDtypeStruct((n, BRANCH_W), F32),
                   jax.ShapeDtypeStruct((n, 2 * D_MODEL), F32), kvt, kvt, kvt,
                   jax.ShapeDtypeStruct((BRANCH_W, n), BF16),
                   jax.ShapeDtypeStruct((n // tm, BRANCH_W, tm), BF16)],
        compiler_params=pltpu.CompilerParams(dimension_semantics=("arbitrary",),
                                             vmem_limit_bytes=VMEM_LIMIT),
        name="inproj_prompt",
    )(x, g, w_bf16, wt_bf16)


def _half_masks():
    lane = lax.broadcasted_iota(jnp.int32, (1, LANES), 1)
    return lane < HEAD_DIM, lane >= HEAD_DIM


def _stack_halves(q):
    lo, hi = _half_masks()
    zero = jnp.zeros_like(q)
    return jnp.concatenate([jnp.where(lo, q, zero), jnp.where(hi, q, zero)], axis=0)


def _strict_upper_ones(n):
    j = lax.broadcasted_iota(jnp.int32, (n, n), 0)
    s = lax.broadcasted_iota(jnp.int32, (n, n), 1)
    return jnp.where(j > s, 1.0, 0.0).astype(BF16)


def _sb_scores(z, mask, tri):
    sp = _softplus(z)
    log_keep = -sp if mask is None else jnp.where(mask, -sp, 0.0)
    return log_keep, z - sp + _split_dot(log_keep, tri)


def _sb_attn_kernel(q_ref, k_ref, v_ref, o_ref, acc_ref, car_ref, *, tq):
    i = pl.program_id(2)
    q2 = _stack_halves(q_ref[...])
    tri = _strict_upper_ones(tq)
    row = lax.broadcasted_iota(jnp.int32, (2 * tq, tq), 0)
    row = jnp.where(row >= tq, row - tq, row)
    col = lax.broadcasted_iota(jnp.int32, (2 * tq, tq), 1)
    acc_ref[...] = jnp.zeros_like(acc_ref)
    car_ref[...] = jnp.zeros_like(car_ref)

    def body(state):
        j, _ = state
        ks = pl.multiple_of(j * tq, tq)
        mask = (col + (j - i) * tq) < row
        log_keep, logit = _sb_scores(_nt_dot(q2, k_ref[pl.ds(ks, tq), :]), mask, tri)
        carry = car_ref[...]
        a = jnp.where(mask, jnp.exp(logit + carry), 0.0)
        acc_ref[...] += _dot(a.astype(BF16), v_ref[pl.ds(ks, tq), :])
        carry = carry + jnp.sum(log_keep, axis=1, keepdims=True)
        car_ref[...] = carry
        return j - 1, jnp.max(carry)

    lax.while_loop(lambda s: (s[0] >= 0) & (s[1] > -SB_STOP), body, (i, jnp.float32(0.0)))
    lo, _ = _half_masks()
    o_ref[...] = jnp.where(lo, acc_ref[:tq, :], acc_ref[tq:, :]).astype(o_ref.dtype)


def _sb_attn(qkv, batch, seq, tq):
    nq = seq // tq
    return pl.pallas_call(
        functools.partial(_sb_attn_kernel, tq=tq),
        grid=(batch, BLK, nq),
        in_specs=[pl.BlockSpec((tq, LANES), lambda b, p, i: (b * nq + i, COL_SQ + p)),
                  pl.BlockSpec((seq, LANES), lambda b, p, i: (b, COL_SK + p)),
                  pl.BlockSpec((seq, LANES), lambda b, p, i: (b, COL_SV + p))],
        out_specs=pl.BlockSpec((tq, LANES), lambda b, p, i: (b * nq + i, p)),
        out_shape=jax.ShapeDtypeStruct((batch * seq, BRANCH_W), BF16),
        scratch_shapes=[pltpu.VMEM((2 * tq, LANES), F32), pltpu.VMEM((2 * tq, 1), F32)],
        compiler_params=pltpu.CompilerParams(dimension_semantics=("arbitrary",) * 3,
                                             vmem_limit_bytes=VMEM_LIMIT),
        name="sb_attn",
    )(qkv, qkv, qkv)


def _lambda(lamv_ref, lambda_init):
    a = jnp.sum(lamv_ref[0:1, :] * lamv_ref[1:2, :], axis=1, keepdims=True)
    b = jnp.sum(lamv_ref[2:3, :] * lamv_ref[3:4, :], axis=1, keepdims=True)
    return jnp.exp(a) - jnp.exp(b) + lambda_init


def _df_attn_kernel(tab_ref, lamv_ref, qt_ref, k_ref, vt_ref, bkt_ref, g_ref, o_ref,
                    bias_s, m_s, l_s, acc_s, *, tq, lambda_init):
    h = pl.program_id(1)
    i = pl.program_id(2)

    @pl.when(i == 0)
    def _():
        far = tab_ref[N_BUCKETS - 1, h]
        for t in range(2):
            bias_s[t] = _bias_from_buckets(bkt_ref[t], lambda b: (tab_ref[b, h] - far) * LOG2E)

    qt = qt_ref[...].astype(F32)
    feat = lax.broadcasted_iota(jnp.int32, qt.shape, 0)
    q2t = jnp.concatenate([jnp.where(feat < HEAD_DIM, qt, 0.0), jnp.where(feat >= HEAD_DIM, qt, 0.0)],
                          axis=1).astype(BF16)
    m_s[...] = jnp.full_like(m_s, NEG_BIG)
    l_s[...] = jnp.zeros_like(l_s)
    acc_s[...] = jnp.zeros_like(acc_s)

    def step(blocks):
        scores = []
        for j, bias, mask in blocks:
            s = _dot(k_ref[pl.ds(pl.multiple_of(j * tq, tq), tq), :], q2t)
            if bias is not None:
                s = s + jnp.concatenate([bias, bias], axis=1)
            if mask is not None:
                s = jnp.where(mask, s, NEG_BIG)
            scores.append(s)
        m_old = m_s[...]
        m_new = m_old
        for s in scores:
            m_new = jnp.maximum(m_new, jnp.max(s, axis=0, keepdims=True))
        alpha = jnp.exp2(m_old - m_new)
        l_new = alpha * l_s[...]
        acc = alpha * acc_s[...]
        for (j, _, _), s in zip(blocks, scores):
            p = jnp.exp2(s - m_new)
            l_new = l_new + jnp.sum(p, axis=0, keepdims=True)
            acc = acc + _dot(vt_ref[j], p.astype(BF16))
        l_s[...] = l_new
        acc_s[...] = acc
        m_s[...] = m_new

    n_far = jnp.maximum(i - 1, 0)

    def far_pair(jj, c):
        step([(2 * jj, None, None), (2 * jj + 1, None, None)])
        return c

    lax.fori_loop(0, n_far // 2, far_pair, 0)

    @pl.when(n_far % 2 == 1)
    def _():
        step([(n_far - 1, None, None)])

    key = lax.broadcasted_iota(jnp.int32, (tq, 2 * tq), 0)
    qry = lax.broadcasted_iota(jnp.int32, (tq, 2 * tq), 1)
    qry = jnp.where(qry >= tq, qry - tq, qry)
    diag = (i, bias_s[0], key <= qry)

    @pl.when(i >= 1)
    def _():
        step([(i - 1, bias_s[1], None), diag])

    @pl.when(i == 0)
    def _():
        step([diag])

    lam = _lambda(lamv_ref, lambda_init)
    norm = acc_s[...] / l_s[...]
    out = (norm[:, :tq] - lam * norm[:, tq:]).T
    o_ref[...] = (_rms(out, g_ref[...]) * (1.0 - lambda_init)).astype(o_ref.dtype)


def _df_attn(tab, lamv, dqt, qkv, dvt, buckets, g_subln, batch, seq, tq, lambda_init):
    nq = seq // tq
    return pl.pallas_call(
        functools.partial(_df_attn_kernel, tq=tq, lambda_init=lambda_init),
        grid=(batch, DF_HEADS, nq),
        in_specs=[pl.BlockSpec(memory_space=pltpu.SMEM),
                  pl.BlockSpec((4, HEAD_DIM), lambda b, h, i: (0, 0)),
                  pl.BlockSpec((LANES, tq), lambda b, h, i: (h, b * nq + i)),
                  pl.BlockSpec((seq, LANES), lambda b, h, i: (b, COL_DK + h)),
                  pl.BlockSpec((nq, LANES, tq), lambda b, h, i: (b, h, 0)),
                  pl.BlockSpec((2, tq, tq), lambda b, h, i: (0, 0, 0)),
                  pl.BlockSpec((1, LANES), lambda b, h, i: (0, 0))],
        out_specs=pl.BlockSpec((tq, LANES), lambda b, h, i: (b * nq + i, h)),
        out_shape=jax.ShapeDtypeStruct((batch * seq, BRANCH_W), BF16),
        scratch_shapes=[pltpu.VMEM((2, tq, tq), F32), pltpu.VMEM((1, 2 * tq), F32),
                        pltpu.VMEM((1, 2 * tq), F32), pltpu.VMEM((LANES, 2 * tq), F32)],
        compiler_params=pltpu.CompilerParams(dimension_semantics=("arbitrary",) * 3,
                                             vmem_limit_bytes=VMEM_LIMIT),
        name="df_attn",
    )(tab, lamv, dqt, qkv, dvt, buckets, g_subln)


def _decode_kernel(pt_ref, tab_ref, lamv_ref, qkv_ref, *rest, dq, page, lambda_init, group):
    ksb, vsb, kdf, vdf = (rest[g * group:(g + 1) * group] for g in range(4))
    bkt_ref, g_ref, osb_ref, odf_ref, qsb_s, qdf_s, bias_s, acc_sb, car_s, acc_df, m_s, l_s = rest[4 * group:]
    s = pl.program_id(1)
    n_steps = pl.num_programs(1)
    rows = SB_HEADS * dq
    hrows = rows // DF_HEADS
    tri = _strict_upper_ones(page)

    def sb_update(z_list, mask, v_list, v_transposed):
        carry = car_s[...]
        acc = acc_sb[...]
        for z, v in zip(z_list, v_list):
            log_keep, logit = _sb_scores(z, mask, tri)
            a = jnp.exp(logit + carry)
            if mask is not None:
                a = jnp.where(mask, a, 0.0)
            a = a.astype(BF16)
            acc = acc + (_nt_dot(a, v) if v_transposed else _dot(a, v))
            carry = carry + jnp.sum(log_keep, axis=1, keepdims=True)
        acc_sb[...] = acc
        car_s[...] = carry

    def df_update(s_list, v_of):
        m_old = m_s[...]
        m_new = m_old
        for sc in s_list:
            m_new = jnp.maximum(m_new, jnp.max(sc, axis=1, keepdims=True))
        alpha = jnp.exp(m_old - m_new)
        p_list = [jnp.exp(sc - m_new) for sc in s_list]
        l_new = alpha * l_s[...]
        for p in p_list:
            l_new = l_new + jnp.sum(p, axis=1, keepdims=True)
        for h in range(DF_HEADS):
            sl = slice(h * hrows, (h + 1) * hrows)
            acc = alpha[sl] * acc_df[sl, :]
            for g, p in enumerate(p_list):
                acc = acc + _dot(p[sl].astype(BF16), v_of(g, h))
            acc_df[sl, :] = acc
        m_s[...] = m_new
        l_s[...] = l_new

    @pl.when(s == 0)
    def _():
        rq = lax.broadcasted_iota(jnp.int32, (rows, BRANCH_W), 0) // dq
        lq = lax.broadcasted_iota(jnp.int32, (rows, BRANCH_W), 1) // HEAD_DIM
        for c, dst in ((0, qsb_s), (3, qdf_s)):
            q = qkv_ref[0, :, c * BRANCH_W:(c + 1) * BRANCH_W].astype(F32)
            qt = jnp.concatenate([q] * SB_HEADS, axis=0)
            dst[...] = jnp.where(rq == lq, qt, 0.0).astype(BF16)
        for h in range(DF_HEADS):
            far = tab_ref[N_BUCKETS - 1, h]
            for t in range(2):
                bias_s[t, h * hrows:(h + 1) * hrows, :] = _bias_from_buckets(
                    bkt_ref[t, h * hrows:(h + 1) * hrows, :], lambda b: tab_ref[b, h] - far)
        acc_sb[...] = jnp.zeros_like(acc_sb)
        car_s[...] = jnp.zeros_like(car_s)
        acc_df[...] = jnp.zeros_like(acc_df)
        m_s[...] = jnp.full_like(m_s, NEG_BIG)
        l_s[...] = jnp.zeros_like(l_s)

        def new_rows(c):
            x = qkv_ref[0, :, c * BRANCH_W:(c + 1) * BRANCH_W].astype(F32)
            return jnp.concatenate([x, jnp.zeros((page - dq, BRANCH_W), F32)], axis=0).astype(BF16)

        row = lax.broadcasted_iota(jnp.int32, (rows, page), 0) % dq
        col = lax.broadcasted_iota(jnp.int32, (rows, page), 1)
        sb_update([_nt_dot(qsb_s[...], new_rows(1))], col < row, [new_rows(2)], False)
        v_new = new_rows(5)
        sc = jnp.where(col <= row, _nt_dot(qdf_s[...], new_rows(4)) + bias_s[0], NEG_BIG)
        df_update([sc], lambda g, h: v_new[:, h * LANES:(h + 1) * LANES])

    def cached_step(newest):
        @pl.when(jnp.max(car_s[...]) > -SB_STOP)
        def _():
            q = qsb_s[...]
            sb_update([_dot(q, k[...].astype(BF16)) for k in ksb], None,
                      [v[...].astype(BF16) for v in vsb], True)

        q = qdf_s[...]
        scores = [_dot(q, k[...].astype(BF16)) for k in kdf]
        if newest:
            scores[0] = scores[0] + bias_s[1]
        df_update(scores, lambda g, h: vdf[g][pl.ds(h, page, stride=DF_HEADS), :].astype(BF16))

    @pl.when(s == 1)
    def _():
        cached_step(True)

    @pl.when(s >= 2)
    def _():
        cached_step(False)

    @pl.when(s == n_steps - 1)
    def _():
        rq = lax.broadcasted_iota(jnp.int32, (rows, BRANCH_W), 0)
        lq = lax.broadcasted_iota(jnp.int32, (rows, BRANCH_W), 1)
        sb = jnp.where(rq // dq == lq // HEAD_DIM, acc_sb[...], 0.0)
        sb = functools.reduce(lambda a, b: a + b, [sb[g * dq:(g + 1) * dq, :] for g in range(SB_HEADS)])
        osb_ref[0] = sb.astype(osb_ref.dtype)

        lam = _lambda(lamv_ref, lambda_init)
        norm = acc_df[...] / l_s[...]
        g = g_ref[...]
        heads = []
        for h in range(DF_HEADS):
            o = norm[h * hrows:h * hrows + dq, :] - lam * norm[h * hrows + dq:(h + 1) * hrows, :]
            heads.append(_rms(o, g))
        odf_ref[0] = (jnp.concatenate(heads, axis=1) * (1.0 - lambda_init)).astype(odf_ref.dtype)


def _decode(page_table, tab, lamv, qkv3, caches, dec_buckets, g_subln, lambda_init, page):
    nb, dq, _ = qkv3.shape
    n_pages = page_table.shape[1]
    group = PAGES_PER_STEP
    while n_pages % group:
        group //= 2
    rows = SB_HEADS * dq

    def cache_spec(g):
        def index(b, s, pt):
            return (pt[b, n_pages - 1 - (jnp.maximum(s - 1, 0) * group + g)], 0)
        return pl.BlockSpec((BRANCH_W, page), index)

    cache_specs = [cache_spec(g) for _ in range(4) for g in range(group)]
    cache_args = [c for c in caches for _ in range(group)]
    out = jax.ShapeDtypeStruct((nb, dq, BRANCH_W), BF16)
    const2 = lambda b, s, pt: (0, 0)
    grid_spec = pltpu.PrefetchScalarGridSpec(
        num_scalar_prefetch=1,
        grid=(nb, n_pages // group + 1),
        in_specs=[pl.BlockSpec(memory_space=pltpu.SMEM),
                  pl.BlockSpec((4, HEAD_DIM), const2),
                  pl.BlockSpec((1, dq, SAMPLE_QKV_W), lambda b, s, pt: (b, 0, 0))]
                 + cache_specs
                 + [pl.BlockSpec((2, rows, page), lambda b, s, pt: (0, 0, 0)),
                    pl.BlockSpec((1, LANES), const2)],
        out_specs=[pl.BlockSpec((1, dq, BRANCH_W), lambda b, s, pt: (b, 0, 0))] * 2,
        scratch_shapes=[pltpu.VMEM((rows, BRANCH_W), BF16), pltpu.VMEM((rows, BRANCH_W), BF16),
                        pltpu.VMEM((2, rows, page), F32),
                        pltpu.VMEM((rows, BRANCH_W), F32), pltpu.VMEM((rows, 1), F32),
                        pltpu.VMEM((rows, LANES), F32), pltpu.VMEM((rows, 1), F32),
                        pltpu.VMEM((rows, 1), F32)],
    )
    return pl.pallas_call(
        functools.partial(_decode_kernel, dq=dq, page=page, lambda_init=lambda_init, group=group),
        grid_spec=grid_spec,
        out_shape=[out, out],
        compiler_params=pltpu.CompilerParams(dimension_semantics=("arbitrary", "arbitrary"),
                                             vmem_limit_bytes=VMEM_LIMIT),
        name="decode_attn",
    )(page_table, tab, lamv, qkv3, *cache_args, dec_buckets, g_subln)


def _post_kernel(sb_ref, df_ref, ga_ref, gb_ref, x_ref, bg_ref, wa_ref, wb_ref, wo_ref, gf_ref,
                 wrt_ref, brt_ref, cnt0_ref, x1_ref, h2_ref, comb_ref, pos_ref, cnt_ref, run_s):
    @pl.when(pl.program_id(0) == 0)
    def _():
        run_s[...] = cnt0_ref[...]

    a = _dot(sb_ref[...], wa_ref[...])
    b = _dot(df_ref[...], wb_ref[...])
    merged = jax.nn.sigmoid(ga_ref[...] + bg_ref[0:1, :]) * a + jax.nn.sigmoid(gb_ref[...] + bg_ref[1:2, :]) * b
    x1 = x_ref[...] + _dot(merged.astype(BF16), wo_ref[...])
    x1_ref[...] = x1
    h2 = _rms(x1, gf_ref[...])
    h2_ref[...] = h2.astype(BF16)

    wr = wrt_ref[...]
    wr_hi = wr.astype(BF16)
    wr_lo = (wr - wr_hi.astype(F32)).astype(BF16)
    h_hi = h2.astype(BF16)
    h_lo = (h2 - h_hi.astype(F32)).astype(BF16)
    logits = _nt_dot(wr_hi, h_hi) + _nt_dot(wr_lo, h_hi) + _nt_dot(wr_hi, h_lo) + brt_ref[...]

    expert = lax.broadcasted_iota(jnp.int32, logits.shape, 0).astype(F32)
    vals, hots = [], []
    for _ in range(TOP_K):
        mx = jnp.max(logits, axis=0, keepdims=True)
        first = jnp.min(jnp.where(logits == mx, expert, float(N_EXPERTS)), axis=0, keepdims=True)
        hot = expert == first
        vals.append(mx)
        hots.append(hot)
        logits = jnp.where(hot, -jnp.inf, logits)
    es = [jnp.exp(v - vals[0]) for v in vals]
    denom = functools.reduce(lambda u, v: u + v, es)
    comb = jnp.zeros(logits.shape, F32)
    routed = jnp.zeros(logits.shape, F32)
    for e, hot in zip(es, hots):
        comb = comb + jnp.where(hot, e / denom, 0.0)
        routed = routed + jnp.where(hot, 1.0, 0.0)
    comb_ref[...] = comb

    tm = logits.shape[1]
    j = lax.broadcasted_iota(jnp.int32, (tm, tm), 0)
    s = lax.broadcasted_iota(jnp.int32, (tm, tm), 1)
    earlier = jnp.where(j < s, 1.0, 0.0).astype(BF16)
    rank = _dot(routed.astype(BF16), earlier) + run_s[...]
    pos_ref[...] = jnp.where(routed > 0.0, rank, -1.0 - rank)
    run = run_s[...] + jnp.sum(routed, axis=1, keepdims=True)
    run_s[...] = run
    cnt_ref[...] = run


def _post(sb, df, gates, x, b_gate, wa, wb, wo, g_ffn, w_router_t, b_router_t, cnt0, tm):
    n = x.shape[0]
    row = lambda i: (i, 0)
    col = lambda i: (0, i)
    const = lambda i: (0, 0)
    routing = jax.ShapeDtypeStruct((N_EXPERTS, n), F32)
    return pl.pallas_call(
        _post_kernel,
        grid=(n // tm,),
        in_specs=[pl.BlockSpec((tm, BRANCH_W), row), pl.BlockSpec((tm, BRANCH_W), row),
                  pl.BlockSpec((tm, D_MODEL), lambda i: (i, 0)), pl.BlockSpec((tm, D_MODEL), lambda i: (i, 1)),
                  pl.BlockSpec((tm, D_MODEL), row),
                  pl.BlockSpec((2, D_MODEL), const),
                  pl.BlockSpec((BRANCH_W, D_MODEL), const), pl.BlockSpec((BRANCH_W, D_MODEL), const),
                  pl.BlockSpec((D_MODEL, D_MODEL), const),
                  pl.BlockSpec((1, D_MODEL), const),
                  pl.BlockSpec((N_EXPERTS, D_MODEL), const), pl.BlockSpec((N_EXPERTS, 1), const),
                  pl.BlockSpec((N_EXPERTS, 1), const)],
        out_specs=[pl.BlockSpec((tm, D_MODEL), row), pl.BlockSpec((tm, D_MODEL), row),
                   pl.BlockSpec((N_EXPERTS, tm), col), pl.BlockSpec((N_EXPERTS, tm), col),
                   pl.BlockSpec((N_EXPERTS, 1), const)],
        out_shape=[jax.ShapeDtypeStruct((n, D_MODEL), F32), jax.ShapeDtypeStruct((n, D_MODEL), BF16),
                   routing, routing, jax.ShapeDtypeStruct((N_EXPERTS, 1), F32)],
        scratch_shapes=[pltpu.VMEM((N_EXPERTS, 1), F32)],
        compiler_params=pltpu.CompilerParams(dimension_semantics=("arbitrary",),
                                             vmem_limit_bytes=VMEM_LIMIT),
        name="post_attn",
    )(sb, df, gates, gates, x, b_gate, wa, wb, wo, g_ffn, w_router_t, b_router_t, cnt0)


ROW_TILE = 256
TOKEN_TILE = 256
SLAB = 128
SLAB_ALIGN = 16


def _expert_rows_kernel(te_ref, q0_ref, wlo_ref, whi_ref, valid_ref,
                        h_ref, pos_ref, wi_ref, bi_ref, wo_ref, bo_ref, o_ref, x_s):
    r = pl.program_id(0)

    @pl.when(valid_ref[r] == 0)
    def _():
        o_ref[...] = jnp.zeros_like(o_ref)

    @pl.when(valid_ref[r] == 1)
    def _():
        e = te_ref[r]
        q0 = q0_ref[r].astype(F32)
        row = lax.broadcasted_iota(jnp.int32, (ROW_TILE, TOKEN_TILE), 0).astype(F32) + q0
        x_s[...] = jnp.zeros_like(x_s)

        def window(w, c):
            rank = pos_ref[e, pl.ds(w, 1), :]
            pick = jnp.where(rank == row, 1.0, 0.0).astype(BF16)
            x_s[...] += _dot(pick, h_ref[pl.ds(pl.multiple_of(w * TOKEN_TILE, TOKEN_TILE), TOKEN_TILE), :])
            return c

        lax.fori_loop(wlo_ref[r], whi_ref[r], window, 0)
        h = x_s[...].astype(BF16)
        gate = jnp.minimum(_dot(h, wi_ref[0, :, :D_FF]) + bi_ref[0, :, :D_FF], SWIGLU_LIMIT)
        up = jnp.clip(_dot(h, wi_ref[0, :, D_FF:]) + bi_ref[0, :, D_FF:], -SWIGLU_LIMIT, SWIGLU_LIMIT)
        act = (up + 1.0) * gate * jax.nn.sigmoid(SWIGLU_ALPHA * gate)
        o_ref[...] = (_dot(act.astype(BF16), wo_ref[0]) + bo_ref[0]).astype(o_ref.dtype)


def _expert_rows(tile_meta, h2, pos3, wi, bi, wo, bo, n_row_tiles):
    n = h2.shape[0]
    once = pl.Buffered(1)
    exp3 = lambda r, te, *_: (te[r], 0, 0)
    grid_spec = pltpu.PrefetchScalarGridSpec(
        num_scalar_prefetch=5,
        grid=(n_row_tiles,),
        in_specs=[pl.BlockSpec((n, D_MODEL), lambda r, *_: (0, 0), pipeline_mode=once),
                  pl.BlockSpec(pos3.shape, lambda r, *_: (0, 0, 0), pipeline_mode=once),
                  pl.BlockSpec((1, D_MODEL, 2 * D_FF), exp3, pipeline_mode=once),
                  pl.BlockSpec((1, 1, 2 * D_FF), exp3),
                  pl.BlockSpec((1, D_FF, D_MODEL), exp3, pipeline_mode=once),
                  pl.BlockSpec((1, 1, D_MODEL), exp3)],
        out_specs=pl.BlockSpec((ROW_TILE, D_MODEL), lambda r, *_: (r, 0)),
        scratch_shapes=[pltpu.VMEM((ROW_TILE, D_MODEL), F32)],
    )
    return pl.pallas_call(
        _expert_rows_kernel,
        grid_spec=grid_spec,
        out_shape=jax.ShapeDtypeStruct((n_row_tiles * ROW_TILE, D_MODEL), BF16),
        compiler_params=pltpu.CompilerParams(dimension_semantics=("arbitrary",),
                                             vmem_limit_bytes=VMEM_LIMIT),
        name="expert_rows",
    )(*tile_meta, h2, pos3, wi, bi, wo, bo)


def _combine_kernel(start_ref, nchunk_ref, comb_ref, pos_ref, off_ref, x1_ref, gfin_ref, rows_hbm, y_ref,
                    buf, sem, *, tile0, last_start):
    t = pl.program_id(0) + tile0

    def to_rows(x):
        pad = jnp.zeros((LANES - N_EXPERTS, x.shape[1]), F32)
        return jnp.concatenate([x, pad], axis=0).T

    comb = to_rows(comb_ref[...])
    rank = to_rows(pos_ref[...])
    dest = rank + off_ref[...]
    slot = lax.broadcasted_iota(jnp.int32, (1, SLAB), 1).astype(F32)

    def chunk(c, acc):
        starts = [start_ref[t * N_EXPERTS + e] + c * SLAB for e in range(N_EXPERTS)]
        copies = []
        for e in range(N_EXPERTS):
            src = pl.multiple_of(jnp.minimum(starts[e], last_start), SLAB_ALIGN)
            cp = pltpu.make_async_copy(rows_hbm.at[pl.ds(src, SLAB), :], buf.at[pl.ds(e * SLAB, SLAB), :], sem)
            cp.start()
            copies.append(cp)
        for cp in copies:
            cp.wait()
        pieces = []
        for e in range(N_EXPERTS):
            rel = dest[:, e:e + 1] - starts[e].astype(F32)
            hit = (rel == slot) & (rank[:, e:e + 1] >= 0.0)
            pieces.append(jnp.where(hit, comb[:, e:e + 1], 0.0).astype(BF16))
        return acc + _dot(jnp.concatenate(pieces, axis=1), buf[...])

    acc = lax.fori_loop(0, nchunk_ref[t], chunk, jnp.zeros((TOKEN_TILE, D_MODEL), F32))
    y_ref[...] = _rms(x1_ref[...] + acc, gfin_ref[...])


def _combine(slab_start, n_chunks, comb_t, pos_t, off_row, x1, g_final, rows, tile0):
    n = x1.shape[0]
    n_rows = rows.shape[0]
    grid_spec = pltpu.PrefetchScalarGridSpec(
        num_scalar_prefetch=2,
        grid=(n // TOKEN_TILE,),
        in_specs=[pl.BlockSpec((N_EXPERTS, TOKEN_TILE), lambda i, *_: (0, i + tile0)),
                  pl.BlockSpec((N_EXPERTS, TOKEN_TILE), lambda i, *_: (0, i + tile0)),
                  pl.BlockSpec((1, LANES), lambda i, *_: (0, 0)),
                  pl.BlockSpec((TOKEN_TILE, D_MODEL), lambda i, *_: (i, 0)),
                  pl.BlockSpec((1, D_MODEL), lambda i, *_: (0, 0)),
                  pl.BlockSpec(memory_space=pl.ANY)],
        out_specs=pl.BlockSpec((TOKEN_TILE, D_MODEL), lambda i, *_: (i, 0)),
        scratch_shapes=[pltpu.VMEM((N_EXPERTS * SLAB, D_MODEL), BF16), pltpu.SemaphoreType.DMA(())],
    )
    return pl.pallas_call(
        functools.partial(_combine_kernel, tile0=tile0, last_start=n_rows - SLAB),
        grid_spec=grid_spec,
        out_shape=jax.ShapeDtypeStruct((n, D_MODEL), F32),
        compiler_params=pltpu.CompilerParams(dimension_semantics=("arbitrary",),
                                             vmem_limit_bytes=VMEM_LIMIT),
        name="combine",
    )(slab_start, n_chunks, comb_t, pos_t, off_row, x1, g_final, rows)


def _routing_tables(pos_t, counts, n_tokens):
    n_win = n_tokens // TOKEN_TILE
    max_tiles = (TOP_K * n_tokens + N_EXPERTS * (ROW_TILE - 1)) // ROW_TILE
    n_row_tiles = max_tiles + 1
    counts = counts.astype(jnp.int32)
    tiles_e = (counts + ROW_TILE - 1) // ROW_TILE
    tile_end = jnp.cumsum(tiles_e)
    tile_start = tile_end - tiles_e
    off = tile_start * ROW_TILE
    r = jnp.arange(n_row_tiles, dtype=jnp.int32)
    te = jnp.minimum(jnp.sum(r[:, None] >= tile_end[None, :], axis=1), N_EXPERTS - 1).astype(jnp.int32)
    valid = (r < tile_end[-1]).astype(jnp.int32)
    q0 = (r - tile_start[te]) * ROW_TILE
    first = pos_t[:, ::TOKEN_TILE]
    before = jnp.where(first < 0, -1.0 - first, first).astype(jnp.int32)
    edges = jnp.concatenate([before, counts[:, None]], axis=1)
    e_lo, e_hi = edges[te, :-1], edges[te, 1:]
    wlo = jnp.sum(e_hi <= q0[:, None], axis=1).astype(jnp.int32) * valid
    whi = jnp.sum(e_lo < (q0 + ROW_TILE)[:, None], axis=1).astype(jnp.int32) * valid
    start = off[:, None] + before
    start_al = (start // SLAB_ALIGN) * SLAB_ALIGN
    need = start - start_al + (edges[:, 1:] - edges[:, :-1])
    n_chunks = jnp.maximum(jnp.max((need + SLAB - 1) // SLAB, axis=0), 1).astype(jnp.int32)
    slab_start = start_al.T.reshape(-1).astype(jnp.int32)
    off_row = jnp.pad(off.astype(F32), (0, LANES - N_EXPERTS))[None, :]
    return (te, q0.astype(jnp.int32), wlo, whi, valid), n_row_tiles, slab_start, n_chunks, off_row


def _bucket_np(n):
    max_exact = N_BUCKETS // 2
    nf = np.maximum(n, max_exact).astype(np.float32)
    large = max_exact + (np.log(nf / np.float32(max_exact)) / np.float32(math.log(MAX_DISTANCE / max_exact))
                         * np.float32(N_BUCKETS - max_exact)).astype(np.int32)
    return np.where(n < max_exact, n, np.minimum(large, N_BUCKETS - 1)).astype(np.int32)


def _tile(n, cap):
    t = cap
    while n % t:
        t //= 2
    return t


def kernel(x_prompt, x_sample, cache_sb_k, cache_sb_v, cache_df_k, cache_df_v, page_table, rel_bias, g_attn, w_in, b_gate, df_lambda_q1, df_lambda_k1, df_lambda_q2, df_lambda_k2, g_subln, w_proj_a, w_proj_b, w_out, g_ffn, w_router, b_router, w_exp_in, b_exp_in, w_exp_out, b_exp_out, g_final):
    depth = w_in.shape[0]
    assert depth == 1, "single-layer step"
    batch, seq, _ = x_prompt.shape
    nb, dq, _ = x_sample.shape
    n_pool, page = cache_sb_k.shape[1], cache_sb_k.shape[2]
    tq = _tile(seq, 256)
    assert tq >= MAX_DISTANCE and page >= MAX_DISTANCE, "bias is constant beyond the two nearest tiles"
    lambda_init = 0.8 - 0.6 * math.exp(-0.3 * 0)

    w = w_in[0]
    sec = lambda c: w[:, c * BRANCH_W:(c + 1) * BRANCH_W]
    w_in_b = w.astype(BF16)
    w_rows = jnp.concatenate([sec(0), sec(1), sec(2), sec(4), sec(5), w[:, 6 * BRANCH_W:]], axis=1).astype(BF16)
    w_cols = jnp.concatenate([sec(1), sec(2), sec(4), sec(3), sec(5)], axis=1).T.astype(BF16)
    wa, wb, wo = w_proj_a[0].astype(BF16), w_proj_b[0].astype(BF16), w_out[0].astype(BF16)
    wi, wo_e = w_exp_in[0].astype(BF16), w_exp_out[0].astype(BF16)
    bi, bo = b_exp_in[0][:, None, :], b_exp_out[0][:, None, :]
    lamv = jnp.stack([df_lambda_q1[0], df_lambda_k1[0], df_lambda_q2[0], df_lambda_k2[0]]).astype(F32)
    g_sub = g_subln[0][None, :]
    g_a, g_f, g_fin = g_attn[0][None, :], g_ffn[0][None, :], g_final[None, :]
    b_r = b_router[0][None, :]

    k_i = np.arange(tq)[:, None]
    q_i = np.arange(tq)[None, :]
    near_buckets = jnp.asarray(_bucket_np(np.stack([np.maximum(q_i - k_i, 0), q_i - k_i + tq])))
    r_i = (np.arange(SB_HEADS * dq) % dq)[:, None]
    c_i = np.arange(page)[None, :]
    dec_buckets = jnp.asarray(_bucket_np(np.stack([np.maximum(r_i - c_i, 0), r_i + page - c_i])))

    xp = x_prompt.reshape(batch * seq, D_MODEL)
    xs = x_sample.reshape(nb * dq, D_MODEL)

    qkv_p, dv_p, gates_p, skt, svt, dkt, dqt, dvt = _inproj_prompt(xp, g_a, w_rows, w_cols, batch, seq, tq)
    qkv_s, sk_s, sv_s, dk_s, dv_s, gates_s = _inproj_sample(xs, g_a, w_in_b, _tile(nb * dq, 256))

    sb_p = _sb_attn(qkv_p, batch, seq, tq)
    df_p = _df_attn(rel_bias, lamv, dqt, qkv_p, dvt, near_buckets, g_sub, batch, seq, tq, lambda_init)

    rows_of_pool = n_pool * BRANCH_W
    caches = [jnp.transpose(cache_sb_k[0], (0, 2, 3, 1)).reshape(rows_of_pool, page),
              jnp.transpose(cache_sb_v[0], (0, 2, 3, 1)).reshape(rows_of_pool, page),
              jnp.transpose(cache_df_k[0], (0, 2, 3, 4, 1)).reshape(rows_of_pool, page),
              cache_df_v[0].reshape(rows_of_pool, LANES)]
    sb_s, df_s = _decode(page_table, rel_bias, lamv, qkv_s.reshape(nb, dq, SAMPLE_QKV_W), caches, dec_buckets,
                         g_sub, lambda_init, page)
    sb_s = sb_s.reshape(nb * dq, BRANCH_W)
    df_s = df_s.reshape(nb * dq, BRANCH_W)

    n_p, n_s = batch * seq, nb * dq
    assert n_p % TOKEN_TILE == 0 and n_s % TOKEN_TILE == 0
    w_rt, b_rt = w_router[0].T, b_router[0][:, None]
    post = functools.partial(_post, b_gate=b_gate[0], wa=wa, wb=wb, wo=wo, g_ffn=g_f, w_router_t=w_rt, b_router_t=b_rt)
    x1_p, h2_p, comb_p, pos_p, cnt_p = post(sb_p, df_p, gates_p, xp, cnt0=jnp.zeros((N_EXPERTS, 1), F32),
                                            tm=_tile(n_p, 512))
    x1_s, h2_s, comb_s, pos_s, cnt_all = post(sb_s, df_s, gates_s, xs, cnt0=cnt_p, tm=_tile(n_s, 512))
    h2 = jnp.concatenate([h2_p, h2_s], axis=0)
    comb_t = jnp.concatenate([comb_p, comb_s], axis=1)
    pos_t = jnp.concatenate([pos_p, pos_s], axis=1)
    n_all = n_p + n_s
    tile_meta, n_row_tiles, slab_start, n_chunks, off_row = _routing_tables(pos_t, cnt_all[:, 0], n_all)
    rows = _expert_rows(tile_meta, h2, pos_t.reshape(N_EXPERTS, n_all // TOKEN_TILE, TOKEN_TILE),
                        wi, bi, wo_e, bo, n_row_tiles)
    y_p = _combine(slab_start, n_chunks, comb_t, pos_t, off_row, x1_p, g_fin, rows, 0)
    y_s = _combine(slab_start, n_chunks, comb_t, pos_t, off_row, x1_s, g_fin, rows, n_p // TOKEN_TILE)
    y_p = y_p.reshape(batch, seq, D_MODEL)
    y_s = y_s.reshape(nb, dq, D_MODEL)

    def from_t(a, feat_shape):
        a = a.reshape((batch,) + feat_shape + (seq,))
        return jnp.moveaxis(a, -1, 1)[None]

    s_lead = (depth, nb, dq)
    return (y_p, y_s,
            from_t(skt, (SB_HEADS, HEAD_DIM)), from_t(svt, (SB_HEADS, HEAD_DIM)),
            from_t(dkt, (DF_HEADS, 2, HEAD_DIM)), dv_p.reshape(depth, batch, seq, DF_HEADS, 2 * HEAD_DIM),
            sk_s.reshape(s_lead + (SB_HEADS, HEAD_DIM)), sv_s.reshape(s_lead + (SB_HEADS, HEAD_DIM)),
            dk_s.reshape(s_lead + (DF_HEADS, 2, HEAD_DIM)), dv_s.reshape(s_lead + (DF_HEADS, 2 * HEAD_DIM)))
```

```python
import functools
import math

import numpy as np
import jax
import jax.numpy as jnp
from jax import lax
from jax.experimental import pallas as pl
from jax.experimental.pallas import tpu as pltpu

F32 = jnp.float32
BF16 = jnp.bfloat16

D_MODEL = 1024
SB_HEADS = 8
DF_HEADS = 4
HEAD_DIM = 64
BRANCH_W = 512
LANES = 128
N_BUCKETS = 32
MAX_DISTANCE = 128
N_EXPERTS = 32
TOP_K = 4
D_FF = 1024
SWIGLU_LIMIT = 7.0
SWIGLU_ALPHA = 1.702
RMS_EPS = 1e-5
QK_SCALE = HEAD_DIM ** -0.5
LOG2E = math.log2(math.e)

BLK = BRANCH_W // LANES
COL_SQ, COL_SK, COL_SV, COL_DK = 0, BLK, 2 * BLK, 3 * BLK
SAMPLE_QKV_W = 6 * BRANCH_W
PROMPT_QKV_W = 4 * BRANCH_W

SB_STOP = 110.0
NEG_BIG = -1e30
VMEM_LIMIT = 56 * 1024 * 1024
PAGES_PER_STEP = 4


def _nt_dot(a, b):
    return lax.dot_general(a, b, (((1,), (1,)), ((), ())), preferred_element_type=F32)


def _dot(a, b):
    return jnp.dot(a, b, preferred_element_type=F32)


def _split_dot(x, w_bf16):
    hi = x.astype(BF16)
    lo = (x - hi.astype(F32)).astype(BF16)
    return _dot(hi, w_bf16) + _dot(lo, w_bf16)


def _softplus(z):
    return jnp.maximum(z, 0.0) + jnp.log1p(jnp.exp(-jnp.abs(z)))


def _rms(x, g):
    return x * lax.rsqrt(jnp.mean(x * x, axis=-1, keepdims=True) + RMS_EPS) * g


def _bias_from_buckets(bucket, value_of):
    out = jnp.zeros(bucket.shape, F32)
    for b in range(N_BUCKETS):
        out = jnp.where(bucket == b, value_of(b), out)
    return out


def _inproj_sample_kernel(x_ref, g_ref, w_ref, qkv_ref, sk_ref, sv_ref, dk_ref, dv_ref, gate_ref):
    h = _rms(x_ref[...], g_ref[...]).astype(BF16)

    def sec(c):
        return _dot(h, w_ref[:, c * BRANCH_W:(c + 1) * BRANCH_W])

    for c, (scale, f32_out) in enumerate(((QK_SCALE, None), (1.0, sk_ref), (1.0, sv_ref),
                                          (QK_SCALE, None), (1.0, dk_ref), (1.0, dv_ref))):
        p = sec(c)
        if f32_out is not None:
            f32_out[...] = p
        qkv_ref[:, c * BRANCH_W:(c + 1) * BRANCH_W] = (p * scale).astype(BF16)
    for c in range(4):
        gate_ref[:, c * BRANCH_W:(c + 1) * BRANCH_W] = sec(6 + c)


def _inproj_sample(x, g, w_bf16, tm):
    n = x.shape[0]
    in_w = w_bf16.shape[1]
    kv = jax.ShapeDtypeStruct((n, BRANCH_W), F32)
    row = lambda i: (i, 0)
    return pl.pallas_call(
        _inproj_sample_kernel,
        grid=(n // tm,),
        in_specs=[pl.BlockSpec((tm, D_MODEL), row),
                  pl.BlockSpec((1, D_MODEL), lambda i: (0, 0)),
                  pl.BlockSpec((D_MODEL, in_w), lambda i: (0, 0))],
        out_specs=[pl.BlockSpec((tm, SAMPLE_QKV_W), row)] + [pl.BlockSpec((tm, BRANCH_W), row)] * 4
                  + [pl.BlockSpec((tm, 2 * D_MODEL), row)],
        out_shape=[jax.ShapeDtypeStruct((n, SAMPLE_QKV_W), BF16), kv, kv, kv, kv,
                   jax.ShapeDtypeStruct((n, 2 * D_MODEL), F32)],
        compiler_params=pltpu.CompilerParams(dimension_semantics=("arbitrary",),
                                             vmem_limit_bytes=VMEM_LIMIT),
        name="inproj_sample",
    )(x, g, w_bf16)


def _inproj_prompt_kernel(x_ref, g_ref, w_ref, wt_ref, qkv_ref, dv_ref, gate_ref,
                          skt_ref, svt_ref, dkt_ref, dqt_ref, dvt_ref):
    h = _rms(x_ref[...], g_ref[...]).astype(BF16)

    def sec(c):
        return _dot(h, w_ref[:, c * BRANCH_W:(c + 1) * BRANCH_W])

    def sec_t(c):
        return _nt_dot(wt_ref[c * BRANCH_W:(c + 1) * BRANCH_W, :], h)

    qkv_ref[:, 0 * BRANCH_W:1 * BRANCH_W] = (sec(0) * QK_SCALE).astype(BF16)
    for c in (1, 2, 3):
        qkv_ref[:, c * BRANCH_W:(c + 1) * BRANCH_W] = sec(c).astype(BF16)
    dv_ref[...] = sec(4)
    for c in range(4):
        gate_ref[:, c * BRANCH_W:(c + 1) * BRANCH_W] = sec(5 + c)
    skt_ref[0] = sec_t(0)
    svt_ref[0] = sec_t(1)
    dkt_ref[0] = sec_t(2)
    dqt_ref[...] = (sec_t(3) * (QK_SCALE * LOG2E)).astype(BF16)
    dvt_ref[0] = sec_t(4).astype(BF16)


def _inproj_prompt(x, g, w_bf16, wt_bf16, batch, seq, tm):
    n = x.shape[0]
    per_seq = seq // tm
    row = lambda i: (i, 0)
    const = lambda i: (0, 0)
    kvt = jax.ShapeDtypeStruct((batch, BRANCH_W, seq), F32)
    kvt_spec = pl.BlockSpec((1, BRANCH_W, tm), lambda i: (i // per_seq, 0, i % per_seq))
    return pl.pallas_call(
        _inproj_prompt_kernel,
        grid=(n // tm,),
        in_specs=[pl.BlockSpec((tm, D_MODEL), row), pl.BlockSpec((1, D_MODEL), const),
                  pl.BlockSpec(w_bf16.shape, const), pl.BlockSpec(wt_bf16.shape, const)],
        out_specs=[pl.BlockSpec((tm, PROMPT_QKV_W), row), pl.BlockSpec((tm, BRANCH_W), row),
                   pl.BlockSpec((tm, 2 * D_MODEL), row), kvt_spec, kvt_spec, kvt_spec,
                   pl.BlockSpec((BRANCH_W, tm), lambda i: (0, i)),
                   pl.BlockSpec((1, BRANCH_W, tm), lambda i: (i, 0, 0))],
        out_shape=[jax.ShapeDtypeStruct((n, PROMPT_QKV_W), BF16), jax.ShapeDtypeStruct((n, BRANCH_W), F32),
                   jax.ShapeDtypeStruct((n, 2 * D_MODEL), F32), kvt, kvt, kvt,
                   jax.ShapeDtypeStruct((BRANCH_W, n), BF16),
                   jax.ShapeDtypeStruct((n // tm, BRANCH_W, tm), BF16)],
        compiler_params=pltpu.CompilerParams(dimension_semantics=("arbitrary",),
                                             vmem_limit_bytes=VMEM_LIMIT),
        name="inproj_prompt",
    )(x, g, w_bf16, wt_bf16)


def _half_masks():
    lane = lax.broadcasted_iota(jnp.int32, (1, LANES), 1)
    return lane < HEAD_DIM, lane >= HEAD_DIM


def _stack_halves(q):
    lo, hi = _half_masks()
    zero = jnp.zeros_like(q)
    return jnp.concatenate([jnp.where(lo, q, zero), jnp.where(hi, q, zero)], axis=0)


def _strict_upper_ones(n):
    j = lax.broadcasted_iota(jnp.int32, (n, n), 0)
    s = lax.broadcasted_iota(jnp.int32, (n, n), 1)
    return jnp.where(j > s, 1.0, 0.0).astype(BF16)


def _sb_scores(z, mask, tri):
    sp = _softplus(z)
    log_keep = -sp if mask is None else jnp.where(mask, -sp, 0.0)
    return log_keep, z - sp + _split_dot(log_keep, tri)


def _sb_attn_kernel(q_ref, k_ref, v_ref, o_ref, acc_ref, car_ref, *, tq):
    i = pl.program_id(2)
    q2 = _stack_halves(q_ref[...])
    tri = _strict_upper_ones(tq)
    row = lax.broadcasted_iota(jnp.int32, (2 * tq, tq), 0)
    row = jnp.where(row >= tq, row - tq, row)
    col = lax.broadcasted_iota(jnp.int32, (2 * tq, tq), 1)
    acc_ref[...] = jnp.zeros_like(acc_ref)
    car_ref[...] = jnp.zeros_like(car_ref)

    def body(state):
        j, _ = state
        ks = pl.multiple_of(j * tq, tq)
        mask = (col + (j - i) * tq) < row
        log_keep, logit = _sb_scores(_nt_dot(q2, k_ref[pl.ds(ks, tq), :]), mask, tri)
        carry = car_ref[...]
        a = jnp.where(mask, jnp.exp(logit + carry), 0.0)
        acc_ref[...] += _dot(a.astype(BF16), v_ref[pl.ds(ks, tq), :])
        carry = carry + jnp.sum(log_keep, axis=1, keepdims=True)
        car_ref[...] = carry
        return j - 1, jnp.max(carry)

    lax.while_loop(lambda s: (s[0] >= 0) & (s[1] > -SB_STOP), body, (i, jnp.float32(0.0)))
    lo, _ = _half_masks()
    o_ref[...] = jnp.where(lo, acc_ref[:tq, :], acc_ref[tq:, :]).astype(o_ref.dtype)


def _sb_attn(qkv, batch, seq, tq):
    nq = seq // tq
    return pl.pallas_call(
        functools.partial(_sb_attn_kernel, tq=tq),
        grid=(batch, BLK, nq),
        in_specs=[pl.BlockSpec((tq, LANES), lambda b, p, i: (b * nq + i, COL_SQ + p)),
                  pl.BlockSpec((seq, LANES), lambda b, p, i: (b, COL_SK + p)),
                  pl.BlockSpec((seq, LANES), lambda b, p, i: (b, COL_SV + p))],
        out_specs=pl.BlockSpec((tq, LANES), lambda b, p, i: (b * nq + i, p)),
        out_shape=jax.ShapeDtypeStruct((batch * seq, BRANCH_W), BF16),
        scratch_shapes=[pltpu.VMEM((2 * tq, LANES), F32), pltpu.VMEM((2 * tq, 1), F32)],
        compiler_params=pltpu.CompilerParams(dimension_semantics=("arbitrary",) * 3,
                                             vmem_limit_bytes=VMEM_LIMIT),
        name="sb_attn",
    )(qkv, qkv, qkv)


def _lambda(lamv_ref, lambda_init):
    a = jnp.sum(lamv_ref[0:1, :] * lamv_ref[1:2, :], axis=1, keepdims=True)
    b = jnp.sum(lamv_ref[2:3, :] * lamv_ref[3:4, :], axis=1, keepdims=True)
    return jnp.exp(a) - jnp.exp(b) + lambda_init


def _df_attn_kernel(tab_ref, lamv_ref, qt_ref, k_ref, vt_ref, bkt_ref, g_ref, o_ref,
                    bias_s, m_s, l_s, acc_s, *, tq, lambda_init):
    h = pl.program_id(1)
    i = pl.program_id(2)

    @pl.when(i == 0)
    def _():
        far = tab_ref[N_BUCKETS - 1, h]
        for t in range(2):
            bias_s[t] = _bias_from_buckets(bkt_ref[t], lambda b: (tab_ref[b, h] - far) * LOG2E)

    qt = qt_ref[...].astype(F32)
    feat = lax.broadcasted_iota(jnp.int32, qt.shape, 0)
    q2t = jnp.concatenate([jnp.where(feat < HEAD_DIM, qt, 0.0), jnp.where(feat >= HEAD_DIM, qt, 0.0)],
                          axis=1).astype(BF16)
    m_s[...] = jnp.full_like(m_s, NEG_BIG)
    l_s[...] = jnp.zeros_like(l_s)
    acc_s[...] = jnp.zeros_like(acc_s)

    def step(blocks):
        scores = []
        for j, bias, mask in blocks:
            s = _dot(k_ref[pl.ds(pl.multiple_of(j * tq, tq), tq), :], q2t)
            if bias is not None:
                s = s + jnp.concatenate([bias, bias], axis=1)
            if mask is not None:
                s = jnp.where(mask, s, NEG_BIG)
            scores.append(s)
        m_old = m_s[...]
        m_new = m_old
        for s in scores:
            m_new = jnp.maximum(m_new, jnp.max(s, axis=0, keepdims=True))
        alpha = jnp.exp2(m_old - m_new)
        l_new = alpha * l_s[...]
        acc = alpha * acc_s[...]
        for (j, _, _), s in zip(blocks, scores):
            p = jnp.exp2(s - m_new)
            l_new = l_new + jnp.sum(p, axis=0, keepdims=True)
            acc = acc + _dot(vt_ref[j], p.astype(BF16))
        l_s[...] = l_new
        acc_s[...] = acc
        m_s[...] = m_new

    n_far = jnp.maximum(i - 1, 0)

    def far_quad(jj, c):
        step([(4 * jj + k, None, None) for k in range(4)])
        return c

    lax.fori_loop(0, n_far // 4, far_quad, 0)
    done = (n_far // 4) * 4

    @pl.when(n_far % 4 >= 2)
    def _():
        step([(done, None, None), (done + 1, None, None)])

    @pl.when(n_far % 2 == 1)
    def _():
        step([(n_far - 1, None, None)])

    key = lax.broadcasted_iota(jnp.int32, (tq, 2 * tq), 0)
    qry = lax.broadcasted_iota(jnp.int32, (tq, 2 * tq), 1)
    qry = jnp.where(qry >= tq, qry - tq, qry)
    diag = (i, bias_s[0], key <= qry)

    @pl.when(i >= 1)
    def _():
        step([(i - 1, bias_s[1], None), diag])

    @pl.when(i == 0)
    def _():
        step([diag])

    lam = _lambda(lamv_ref, lambda_init)
    norm = acc_s[...] / l_s[...]
    out = (norm[:, :tq] - lam * norm[:, tq:]).T
    o_ref[...] = (_rms(out, g_ref[...]) * (1.0 - lambda_init)).astype(o_ref.dtype)


def _df_attn(tab, lamv, dqt, qkv, dvt, buckets, g_subln, batch, seq, tq, lambda_init):
    nq = seq // tq
    return pl.pallas_call(
        functools.partial(_df_attn_kernel, tq=tq, lambda_init=lambda_init),
        grid=(batch, DF_HEADS, nq),
        in_specs=[pl.BlockSpec(memory_space=pltpu.SMEM),
                  pl.BlockSpec((4, HEAD_DIM), lambda b, h, i: (0, 0)),
                  pl.BlockSpec((LANES, tq), lambda b, h, i: (h, b * nq + i)),
                  pl.BlockSpec((seq, LANES), lambda b, h, i: (b, COL_DK + h)),
                  pl.BlockSpec((nq, LANES, tq), lambda b, h, i: (b, h, 0)),
                  pl.BlockSpec((2, tq, tq), lambda b, h, i: (0, 0, 0)),
                  pl.BlockSpec((1, LANES), lambda b, h, i: (0, 0))],
        out_specs=pl.BlockSpec((tq, LANES), lambda b, h, i: (b * nq + i, h)),
        out_shape=jax.ShapeDtypeStruct((batch * seq, BRANCH_W), BF16),
        scratch_shapes=[pltpu.VMEM((2, tq, tq), F32), pltpu.VMEM((1, 2 * tq), F32),
                        pltpu.VMEM((1, 2 * tq), F32), pltpu.VMEM((LANES, 2 * tq), F32)],
        compiler_params=pltpu.CompilerParams(dimension_semantics=("arbitrary",) * 3,
                                             vmem_limit_bytes=VMEM_LIMIT),
        name="df_attn",
    )(tab, lamv, dqt, qkv, dvt, buckets, g_subln)


def _decode_kernel(pt_ref, tab_ref, lamv_ref, qkv_ref, *rest, dq, page, lambda_init, group):
    ksb, vsb, kdf, vdf = (rest[g * group:(g + 1) * group] for g in range(4))
    bkt_ref, g_ref, osb_ref, odf_ref, qsb_s, qdf_s, bias_s, acc_sb, car_s, acc_df, m_s, l_s = rest[4 * group:]
    s = pl.program_id(1)
    n_steps = pl.num_programs(1)
    rows = SB_HEADS * dq
    hrows = rows // DF_HEADS
    tri = _strict_upper_ones(page)

    def sb_update(z_list, mask, v_list, v_transposed):
        carry = car_s[...]
        acc = acc_sb[...]
        for z, v in zip(z_list, v_list):
            log_keep, logit = _sb_scores(z, mask, tri)
            a = jnp.exp(logit + carry)
            if mask is not None:
                a = jnp.where(mask, a, 0.0)
            a = a.astype(BF16)
            acc = acc + (_nt_dot(a, v) if v_transposed else _dot(a, v))
            carry = carry + jnp.sum(log_keep, axis=1, keepdims=True)
        acc_sb[...] = acc
        car_s[...] = carry

    def df_update(s_list, v_of):
        m_old = m_s[...]
        m_new = m_old
        for sc in s_list:
            m_new = jnp.maximum(m_new, jnp.max(sc, axis=1, keepdims=True))
        alpha = jnp.exp(m_old - m_new)
        p_list = [jnp.exp(sc - m_new) for sc in s_list]
        l_new = alpha * l_s[...]
        for p in p_list:
            l_new = l_new + jnp.sum(p, axis=1, keepdims=True)
        for h in range(DF_HEADS):
            sl = slice(h * hrows, (h + 1) * hrows)
            acc = alpha[sl] * acc_df[sl, :]
            for g, p in enumerate(p_list):
                acc = acc + _dot(p[sl].astype(BF16), v_of(g, h))
            acc_df[sl, :] = acc
        m_s[...] = m_new
        l_s[...] = l_new

    @pl.when(s == 0)
    def _():
        rq = lax.broadcasted_iota(jnp.int32, (rows, BRANCH_W), 0) // dq
        lq = lax.broadcasted_iota(jnp.int32, (rows, BRANCH_W), 1) // HEAD_DIM
        for c, dst in ((0, qsb_s), (3, qdf_s)):
            q = qkv_ref[0, :, c * BRANCH_W:(c + 1) * BRANCH_W].astype(F32)
            qt = jnp.concatenate([q] * SB_HEADS, axis=0)
            dst[...] = jnp.where(rq == lq, qt, 0.0).astype(BF16)
        for h in range(DF_HEADS):
            far = tab_ref[N_BUCKETS - 1, h]
            for t in range(2):
                bias_s[t, h * hrows:(h + 1) * hrows, :] = _bias_from_buckets(
                    bkt_ref[t, h * hrows:(h + 1) * hrows, :], lambda b: tab_ref[b, h] - far)
        acc_sb[...] = jnp.zeros_like(acc_sb)
        car_s[...] = jnp.zeros_like(car_s)
        acc_df[...] = jnp.zeros_like(acc_df)
        m_s[...] = jnp.full_like(m_s, NEG_BIG)
        l_s[...] = jnp.zeros_like(l_s)

        def new_rows(c):
            x = qkv_ref[0, :, c * BRANCH_W:(c + 1) * BRANCH_W].astype(F32)
            return jnp.concatenate([x, jnp.zeros((page - dq, BRANCH_W), F32)], axis=0).astype(BF16)

        row = lax.broadcasted_iota(jnp.int32, (rows, page), 0) % dq
        col = lax.broadcasted_iota(jnp.int32, (rows, page), 1)
        sb_update([_nt_dot(qsb_s[...], new_rows(1))], col < row, [new_rows(2)], False)
        v_new = new_rows(5)
        sc = jnp.where(col <= row, _nt_dot(qdf_s[...], new_rows(4)) + bias_s[0], NEG_BIG)
        df_update([sc], lambda g, h: v_new[:, h * LANES:(h + 1) * LANES])

    def cached_step(newest):
        @pl.when(jnp.max(car_s[...]) > -SB_STOP)
        def _():
            q = qsb_s[...]
            sb_update([_dot(q, k[...].astype(BF16)) for k in ksb], None,
                      [v[...].astype(BF16) for v in vsb], True)

        q = qdf_s[...]
        scores = [_dot(q, k[...].astype(BF16)) for k in kdf]
        if newest:
            scores[0] = scores[0] + bias_s[1]
        df_update(scores, lambda g, h: vdf[g][pl.ds(h, page, stride=DF_HEADS), :].astype(BF16))

    @pl.when(s == 1)
    def _():
        cached_step(True)

    @pl.when(s >= 2)
    def _():
        cached_step(False)

    @pl.when(s == n_steps - 1)
    def _():
        rq = lax.broadcasted_iota(jnp.int32, (rows, BRANCH_W), 0)
        lq = lax.broadcasted_iota(jnp.int32, (rows, BRANCH_W), 1)
        sb = jnp.where(rq // dq == lq // HEAD_DIM, acc_sb[...], 0.0)
        sb = functools.reduce(lambda a, b: a + b, [sb[g * dq:(g + 1) * dq, :] for g in range(SB_HEADS)])
        osb_ref[0] = sb.astype(osb_ref.dtype)

        lam = _lambda(lamv_ref, lambda_init)
        norm = acc_df[...] / l_s[...]
        g = g_ref[...]
        heads = []
        for h in range(DF_HEADS):
            o = norm[h * hrows:h * hrows + dq, :] - lam * norm[h * hrows + dq:(h + 1) * hrows, :]
            heads.append(_rms(o, g))
        odf_ref[0] = (jnp.concatenate(heads, axis=1) * (1.0 - lambda_init)).astype(odf_ref.dtype)


def _decode(page_table, tab, lamv, qkv3, caches, dec_buckets, g_subln, lambda_init, page):
    nb, dq, _ = qkv3.shape
    n_pages = page_table.shape[1]
    group = PAGES_PER_STEP
    while n_pages % group:
        group //= 2
    rows = SB_HEADS * dq

    def cache_spec(g):
        def index(b, s, pt):
            return (pt[b, n_pages - 1 - (jnp.maximum(s - 1, 0) * group + g)], 0)
        return pl.BlockSpec((BRANCH_W, page), index)

    cache_specs = [cache_spec(g) for _ in range(4) for g in range(group)]
    cache_args = [c for c in caches for _ in range(group)]
    out = jax.ShapeDtypeStruct((nb, dq, BRANCH_W), BF16)
    const2 = lambda b, s, pt: (0, 0)
    grid_spec = pltpu.PrefetchScalarGridSpec(
        num_scalar_prefetch=1,
        grid=(nb, n_pages // group + 1),
        in_specs=[pl.BlockSpec(memory_space=pltpu.SMEM),
                  pl.BlockSpec((4, HEAD_DIM), const2),
                  pl.BlockSpec((1, dq, SAMPLE_QKV_W), lambda b, s, pt: (b, 0, 0))]
                 + cache_specs
                 + [pl.BlockSpec((2, rows, page), lambda b, s, pt: (0, 0, 0)),
                    pl.BlockSpec((1, LANES), const2)],
        out_specs=[pl.BlockSpec((1, dq, BRANCH_W), lambda b, s, pt: (b, 0, 0))] * 2,
        scratch_shapes=[pltpu.VMEM((rows, BRANCH_W), BF16), pltpu.VMEM((rows, BRANCH_W), BF16),
                        pltpu.VMEM((2, rows, page), F32),
                        pltpu.VMEM((rows, BRANCH_W), F32), pltpu.VMEM((rows, 1), F32),
                        pltpu.VMEM((rows, LANES), F32), pltpu.VMEM((rows, 1), F32),
                        pltpu.VMEM((rows, 1), F32)],
    )
    return pl.pallas_call(
        functools.partial(_decode_kernel, dq=dq, page=page, lambda_init=lambda_init, group=group),
        grid_spec=grid_spec,
        out_shape=[out, out],
        compiler_params=pltpu.CompilerParams(dimension_semantics=("arbitrary", "arbitrary"),
                                             vmem_limit_bytes=VMEM_LIMIT),
        name="decode_attn",
    )(page_table, tab, lamv, qkv3, *cache_args, dec_buckets, g_subln)


def _post_kernel(sb_ref, df_ref, ga_ref, gb_ref, x_ref, bg_ref, wa_ref, wb_ref, wo_ref, gf_ref,
                 wrt_ref, brt_ref, cnt0_ref, x1_ref, h2_ref, comb_ref, pos_ref, cnt_ref, run_s):
    @pl.when(pl.program_id(0) == 0)
    def _():
        run_s[...] = cnt0_ref[...]

    a = _dot(sb_ref[...], wa_ref[...])
    b = _dot(df_ref[...], wb_ref[...])
    merged = jax.nn.sigmoid(ga_ref[...] + bg_ref[0:1, :]) * a + jax.nn.sigmoid(gb_ref[...] + bg_ref[1:2, :]) * b
    x1 = x_ref[...] + _dot(merged.astype(BF16), wo_ref[...])
    x1_ref[...] = x1
    h2 = _rms(x1, gf_ref[...])
    h2_ref[...] = h2.astype(BF16)

    wr = wrt_ref[...]
    wr_hi = wr.astype(BF16)
    wr_lo = (wr - wr_hi.astype(F32)).astype(BF16)
    h_hi = h2.astype(BF16)
    h_lo = (h2 - h_hi.astype(F32)).astype(BF16)
    logits = _nt_dot(wr_hi, h_hi) + _nt_dot(wr_lo, h_hi) + _nt_dot(wr_hi, h_lo) + brt_ref[...]

    expert = lax.broadcasted_iota(jnp.int32, logits.shape, 0).astype(F32)
    vals, hots = [], []
    for _ in range(TOP_K):
        mx = jnp.max(logits, axis=0, keepdims=True)
        first = jnp.min(jnp.where(logits == mx, expert, float(N_EXPERTS)), axis=0, keepdims=True)
        hot = expert == first
        vals.append(mx)
        hots.append(hot)
        logits = jnp.where(hot, -jnp.inf, logits)
    es = [jnp.exp(v - vals[0]) for v in vals]
    denom = functools.reduce(lambda u, v: u + v, es)
    comb = jnp.zeros(logits.shape, F32)
    routed = jnp.zeros(logits.shape, F32)
    for e, hot in zip(es, hots):
        comb = comb + jnp.where(hot, e / denom, 0.0)
        routed = routed + jnp.where(hot, 1.0, 0.0)
    comb_ref[...] = comb

    tm = logits.shape[1]
    j = lax.broadcasted_iota(jnp.int32, (tm, tm), 0)
    s = lax.broadcasted_iota(jnp.int32, (tm, tm), 1)
    earlier = jnp.where(j < s, 1.0, 0.0).astype(BF16)
    rank = _dot(routed.astype(BF16), earlier) + run_s[...]
    pos_ref[...] = jnp.where(routed > 0.0, rank, -1.0 - rank)
    run = run_s[...] + jnp.sum(routed, axis=1, keepdims=True)
    run_s[...] = run
    cnt_ref[...] = run


def _post(sb, df, gates, x, b_gate, wa, wb, wo, g_ffn, w_router_t, b_router_t, cnt0, tm):
    n = x.shape[0]
    row = lambda i: (i, 0)
    col = lambda i: (0, i)
    const = lambda i: (0, 0)
    routing = jax.ShapeDtypeStruct((N_EXPERTS, n), F32)
    return pl.pallas_call(
        _post_kernel,
        grid=(n // tm,),
        in_specs=[pl.BlockSpec((tm, BRANCH_W), row), pl.BlockSpec((tm, BRANCH_W), row),
                  pl.BlockSpec((tm, D_MODEL), lambda i: (i, 0)), pl.BlockSpec((tm, D_MODEL), lambda i: (i, 1)),
                  pl.BlockSpec((tm, D_MODEL), row),
                  pl.BlockSpec((2, D_MODEL), const),
                  pl.BlockSpec((BRANCH_W, D_MODEL), const), pl.BlockSpec((BRANCH_W, D_MODEL), const),
                  pl.BlockSpec((D_MODEL, D_MODEL), const),
                  pl.BlockSpec((1, D_MODEL), const),
                  pl.BlockSpec((N_EXPERTS, D_MODEL), const), pl.BlockSpec((N_EXPERTS, 1), const),
                  pl.BlockSpec((N_EXPERTS, 1), const)],
        out_specs=[pl.BlockSpec((tm, D_MODEL), row), pl.BlockSpec((tm, D_MODEL), row),
                   pl.BlockSpec((N_EXPERTS, tm), col), pl.BlockSpec((N_EXPERTS, tm), col),
                   pl.BlockSpec((N_EXPERTS, 1), const)],
        out_shape=[jax.ShapeDtypeStruct((n, D_MODEL), F32), jax.ShapeDtypeStruct((n, D_MODEL), BF16),
                   routing, routing, jax.ShapeDtypeStruct((N_EXPERTS, 1), F32)],
        scratch_shapes=[pltpu.VMEM((N_EXPERTS, 1), F32)],
        compiler_params=pltpu.CompilerParams(dimension_semantics=("arbitrary",),
                                             vmem_limit_bytes=VMEM_LIMIT),
        name="post_attn",
    )(sb, df, gates, gates, x, b_gate, wa, wb, wo, g_ffn, w_router_t, b_router_t, cnt0)


ROW_TILE = 256
TOKEN_TILE = 256
SLAB = 64
GATHER_SPAN = 12
SLAB_ALIGN = 16


def _expert_rows_kernel(te_ref, q0_ref, wlo_ref, whi_ref, valid_ref,
                        h_ref, pos_ref, comb_ref, wi_ref, bi_ref, wo_ref, bo_ref, o_ref, x_s, g_s, *, span):
    r = pl.program_id(0)

    @pl.when(valid_ref[r] == 0)
    def _():
        o_ref[...] = jnp.zeros_like(o_ref)

    @pl.when(valid_ref[r] == 1)
    def _():
        e = te_ref[r]
        q0 = q0_ref[r].astype(F32)
        wlo, whi = wlo_ref[r], whi_ref[r]
        n_win = pos_ref.shape[1]
        row = lax.broadcasted_iota(jnp.int32, (ROW_TILE, TOKEN_TILE), 0).astype(F32) + q0

        def picks(w0, k):
            hits = [pos_ref[e, pl.ds(w0 + i, 1), :] == row for i in range(k)]
            gsum = functools.reduce(lambda a, b: a + b,
                                    [jnp.where(h, comb_ref[e, pl.ds(w0 + i, 1), :], 0.0) for i, h in enumerate(hits)])
            return [jnp.where(h, 1.0, 0.0).astype(BF16) for h in hits], gsum

        @pl.when(whi - wlo <= span)
        def _():
            w0 = jnp.minimum(wlo, n_win - span)
            onehot, gsum = picks(w0, span)
            tok0 = pl.multiple_of(w0 * TOKEN_TILE, TOKEN_TILE)
            x_s[...] = _dot(jnp.concatenate(onehot, axis=1), h_ref[pl.ds(tok0, span * TOKEN_TILE), :])
            g_s[...] = gsum

        @pl.when(whi - wlo > span)
        def _():
            x_s[...] = jnp.zeros_like(x_s)
            g_s[...] = jnp.zeros_like(g_s)

            def window(w, c):
                onehot, gsum = picks(w, 1)
                x_s[...] += _dot(onehot[0], h_ref[pl.ds(pl.multiple_of(w * TOKEN_TILE, TOKEN_TILE), TOKEN_TILE), :])
                g_s[...] += gsum
                return c

            lax.fori_loop(wlo, whi, window, 0)

        h = x_s[...].astype(BF16)
        gate = jnp.minimum(_dot(h, wi_ref[0, :, :D_FF]) + bi_ref[0, :, :D_FF], SWIGLU_LIMIT)
        up = jnp.clip(_dot(h, wi_ref[0, :, D_FF:]) + bi_ref[0, :, D_FF:], -SWIGLU_LIMIT, SWIGLU_LIMIT)
        act = (up + 1.0) * gate * jax.nn.sigmoid(SWIGLU_ALPHA * gate)
        weight = jnp.sum(g_s[...], axis=1, keepdims=True)
        o_ref[...] = (weight * (_dot(act.astype(BF16), wo_ref[0]) + bo_ref[0])).astype(o_ref.dtype)


def _expert_rows(tile_meta, h2, pos3, comb3, wi, bi, wo, bo, n_row_tiles):
    n = h2.shape[0]
    once = pl.Buffered(1)
    exp3 = lambda r, te, *_: (te[r], 0, 0)
    span = min(GATHER_SPAN, pos3.shape[1])
    grid_spec = pltpu.PrefetchScalarGridSpec(
        num_scalar_prefetch=5,
        grid=(n_row_tiles,),
        in_specs=[pl.BlockSpec((n, D_MODEL), lambda r, *_: (0, 0), pipeline_mode=once),
                  pl.BlockSpec(pos3.shape, lambda r, *_: (0, 0, 0), pipeline_mode=once),
                  pl.BlockSpec(comb3.shape, lambda r, *_: (0, 0, 0), pipeline_mode=once),
                  pl.BlockSpec((1, D_MODEL, 2 * D_FF), exp3, pipeline_mode=once),
                  pl.BlockSpec((1, 1, 2 * D_FF), exp3),
                  pl.BlockSpec((1, D_FF, D_MODEL), exp3, pipeline_mode=once),
                  pl.BlockSpec((1, 1, D_MODEL), exp3)],
        out_specs=pl.BlockSpec((ROW_TILE, D_MODEL), lambda r, *_: (r, 0)),
        scratch_shapes=[pltpu.VMEM((ROW_TILE, D_MODEL), F32), pltpu.VMEM((ROW_TILE, TOKEN_TILE), F32)],
    )
    return pl.pallas_call(
        functools.partial(_expert_rows_kernel, span=span),
        grid_spec=grid_spec,
        out_shape=jax.ShapeDtypeStruct((n_row_tiles * ROW_TILE, D_MODEL), BF16),
        compiler_params=pltpu.CompilerParams(dimension_semantics=("arbitrary",),
                                             vmem_limit_bytes=VMEM_LIMIT),
        name="expert_rows",
    )(*tile_meta, h2, pos3, comb3, wi, bi, wo, bo)


def _combine_kernel(start_ref, nchunk_ref, pos_ref, off_ref, x1_ref, gfin_ref, rows_hbm, y_ref,
                    buf, sem, *, tile0, last_start):
    t = pl.program_id(0) + tile0
    pad = jnp.zeros((LANES - N_EXPERTS, TOKEN_TILE), F32)
    rank = jnp.concatenate([pos_ref[...], pad], axis=0).T
    dest = jnp.where(rank >= 0.0, rank + off_ref[...], -1.0)
    lane = lax.broadcasted_iota(jnp.int32, (1, 2 * SLAB), 1).astype(F32)

    def chunk(c, acc):
        starts = [start_ref[t * N_EXPERTS + e] + c * SLAB for e in range(N_EXPERTS)]
        copies = []
        for e in range(N_EXPERTS):
            src = pl.multiple_of(jnp.minimum(starts[e], last_start), SLAB_ALIGN)
            cp = pltpu.make_async_copy(rows_hbm.at[pl.ds(src, SLAB), :], buf.at[pl.ds(e * SLAB, SLAB), :], sem)
            cp.start()
            copies.append(cp)
        for cp in copies:
            cp.wait()
        pieces = []
        for e in range(0, N_EXPERTS, 2):
            rel = []
            for k in range(2):
                d = dest[:, e + k:e + k + 1] - starts[e + k].astype(F32)
                rel.append(jnp.where((d >= 0.0) & (d < SLAB), d + k * SLAB, -1.0))
            hit = (rel[0] == lane) | (rel[1] == lane)
            pieces.append(jnp.where(hit, 1.0, 0.0).astype(BF16))
        return acc + _dot(jnp.concatenate(pieces, axis=1), buf[...])

    acc = lax.fori_loop(0, nchunk_ref[t], chunk, jnp.zeros((TOKEN_TILE, D_MODEL), F32))
    y_ref[...] = _rms(x1_ref[...] + acc, gfin_ref[...])


def _combine(slab_start, n_chunks, pos_t, off_row, x1, g_final, rows, tile0):
    n = x1.shape[0]
    n_rows = rows.shape[0]
    grid_spec = pltpu.PrefetchScalarGridSpec(
        num_scalar_prefetch=2,
        grid=(n // TOKEN_TILE,),
        in_specs=[pl.BlockSpec((N_EXPERTS, TOKEN_TILE), lambda i, *_: (0, i + tile0)),
                  pl.BlockSpec((1, LANES), lambda i, *_: (0, 0)),
                  pl.BlockSpec((TOKEN_TILE, D_MODEL), lambda i, *_: (i, 0)),
                  pl.BlockSpec((1, D_MODEL), lambda i, *_: (0, 0)),
                  pl.BlockSpec(memory_space=pl.ANY)],
        out_specs=pl.BlockSpec((TOKEN_TILE, D_MODEL), lambda i, *_: (i, 0)),
        scratch_shapes=[pltpu.VMEM((N_EXPERTS * SLAB, D_MODEL), BF16), pltpu.SemaphoreType.DMA(())],
    )
    return pl.pallas_call(
        functools.partial(_combine_kernel, tile0=tile0, last_start=n_rows - SLAB),
        grid_spec=grid_spec,
        out_shape=jax.ShapeDtypeStruct((n, D_MODEL), F32),
        compiler_params=pltpu.CompilerParams(dimension_semantics=("arbitrary",),
                                             vmem_limit_bytes=VMEM_LIMIT),
        name="combine",
    )(slab_start, n_chunks, pos_t, off_row, x1, g_final, rows)


def _routing_tables(pos_t, counts, n_tokens):
    n_win = n_tokens // TOKEN_TILE
    max_tiles = (TOP_K * n_tokens + N_EXPERTS * (ROW_TILE - 1)) // ROW_TILE
    n_row_tiles = max_tiles + 1
    counts = counts.astype(jnp.int32)
    tiles_e = (counts + ROW_TILE - 1) // ROW_TILE
    tile_end = jnp.cumsum(tiles_e)
    tile_start = tile_end - tiles_e
    off = tile_start * ROW_TILE
    r = jnp.arange(n_row_tiles, dtype=jnp.int32)
    te = jnp.minimum(jnp.sum(r[:, None] >= tile_end[None, :], axis=1), N_EXPERTS - 1).astype(jnp.int32)
    valid = (r < tile_end[-1]).astype(jnp.int32)
    mine = (te[:, None] == jnp.arange(N_EXPERTS, dtype=jnp.int32)[None, :]).astype(jnp.int32)
    q0 = (r - jnp.sum(mine * tile_start[None, :], axis=1)) * ROW_TILE
    first = pos_t[:, ::TOKEN_TILE]
    before = jnp.where(first < 0, -1.0 - first, first).astype(jnp.int32)
    edges = jnp.concatenate([before, counts[:, None]], axis=1)
    edges_r = jnp.sum(mine[:, :, None] * edges[None, :, :], axis=1)
    e_lo, e_hi = edges_r[:, :-1], edges_r[:, 1:]
    wlo = jnp.sum(e_hi <= q0[:, None], axis=1).astype(jnp.int32) * valid
    whi = jnp.sum(e_lo < (q0 + ROW_TILE)[:, None], axis=1).astype(jnp.int32) * valid
    start = off[:, None] + before
    start_al = (start // SLAB_ALIGN) * SLAB_ALIGN
    need = start - start_al + (edges[:, 1:] - edges[:, :-1])
    n_chunks = jnp.maximum(jnp.max((need + SLAB - 1) // SLAB, axis=0), 1).astype(jnp.int32)
    slab_start = start_al.T.reshape(-1).astype(jnp.int32)
    off_row = jnp.pad(off.astype(F32), (0, LANES - N_EXPERTS))[None, :]
    return (te, q0.astype(jnp.int32), wlo, whi, valid), n_row_tiles, slab_start, n_chunks, off_row


def _bucket_np(n):
    max_exact = N_BUCKETS // 2
    nf = np.maximum(n, max_exact).astype(np.float32)
    large = max_exact + (np.log(nf / np.float32(max_exact)) / np.float32(math.log(MAX_DISTANCE / max_exact))
                         * np.float32(N_BUCKETS - max_exact)).astype(np.int32)
    return np.where(n < max_exact, n, np.minimum(large, N_BUCKETS - 1)).astype(np.int32)


def _tile(n, cap):
    t = cap
    while n % t:
        t //= 2
    return t


def kernel(x_prompt, x_sample, cache_sb_k, cache_sb_v, cache_df_k, cache_df_v, page_table, rel_bias, g_attn, w_in, b_gate, df_lambda_q1, df_lambda_k1, df_lambda_q2, df_lambda_k2, g_subln, w_proj_a, w_proj_b, w_out, g_ffn, w_router, b_router, w_exp_in, b_exp_in, w_exp_out, b_exp_out, g_final):
    depth = w_in.shape[0]
    assert depth == 1, "single-layer step"
    batch, seq, _ = x_prompt.shape
    nb, dq, _ = x_sample.shape
    n_pool, page = cache_sb_k.shape[1], cache_sb_k.shape[2]
    tq = _tile(seq, 256)
    assert tq >= MAX_DISTANCE and page >= MAX_DISTANCE, "bias is constant beyond the two nearest tiles"
    lambda_init = 0.8 - 0.6 * math.exp(-0.3 * 0)

    w = w_in[0]
    sec = lambda c: w[:, c * BRANCH_W:(c + 1) * BRANCH_W]
    w_in_b = w.astype(BF16)
    w_rows = jnp.concatenate([sec(0), sec(1), sec(2), sec(4), sec(5), w[:, 6 * BRANCH_W:]], axis=1).astype(BF16)
    w_cols = jnp.concatenate([sec(1), sec(2), sec(4), sec(3), sec(5)], axis=1).T.astype(BF16)
    wa, wb, wo = w_proj_a[0].astype(BF16), w_proj_b[0].astype(BF16), w_out[0].astype(BF16)
    wi, wo_e = w_exp_in[0].astype(BF16), w_exp_out[0].astype(BF16)
    bi, bo = b_exp_in[0][:, None, :], b_exp_out[0][:, None, :]
    lamv = jnp.stack([df_lambda_q1[0], df_lambda_k1[0], df_lambda_q2[0], df_lambda_k2[0]]).astype(F32)
    g_sub = g_subln[0][None, :]
    g_a, g_f, g_fin = g_attn[0][None, :], g_ffn[0][None, :], g_final[None, :]
    b_r = b_router[0][None, :]

    k_i = np.arange(tq)[:, None]
    q_i = np.arange(tq)[None, :]
    near_buckets = jnp.asarray(_bucket_np(np.stack([np.maximum(q_i - k_i, 0), q_i - k_i + tq])))
    r_i = (np.arange(SB_HEADS * dq) % dq)[:, None]
    c_i = np.arange(page)[None, :]
    dec_buckets = jnp.asarray(_bucket_np(np.stack([np.maximum(r_i - c_i, 0), r_i + page - c_i])))

    xp = x_prompt.reshape(batch * seq, D_MODEL)
    xs = x_sample.reshape(nb * dq, D_MODEL)

    qkv_p, dv_p, gates_p, skt, svt, dkt, dqt, dvt = _inproj_prompt(xp, g_a, w_rows, w_cols, batch, seq, tq)
    qkv_s, sk_s, sv_s, dk_s, dv_s, gates_s = _inproj_sample(xs, g_a, w_in_b, _tile(nb * dq, 256))

    sb_p = _sb_attn(qkv_p, batch, seq, tq)
    df_p = _df_attn(rel_bias, lamv, dqt, qkv_p, dvt, near_buckets, g_sub, batch, seq, tq, lambda_init)

    rows_of_pool = n_pool * BRANCH_W
    caches = [jnp.transpose(cache_sb_k[0], (0, 2, 3, 1)).reshape(rows_of_pool, page),
              jnp.transpose(cache_sb_v[0], (0, 2, 3, 1)).reshape(rows_of_pool, page),
              jnp.transpose(cache_df_k[0], (0, 2, 3, 4, 1)).reshape(rows_of_pool, page),
              cache_df_v[0].reshape(rows_of_pool, LANES)]
    sb_s, df_s = _decode(page_table, rel_bias, lamv, qkv_s.reshape(nb, dq, SAMPLE_QKV_W), caches, dec_buckets,
                         g_sub, lambda_init, page)
    sb_s = sb_s.reshape(nb * dq, BRANCH_W)
    df_s = df_s.reshape(nb * dq, BRANCH_W)

    n_p, n_s = batch * seq, nb * dq
    assert n_p % TOKEN_TILE == 0 and n_s % TOKEN_TILE == 0
    w_rt, b_rt = w_router[0].T, b_router[0][:, None]
    post = functools.partial(_post, b_gate=b_gate[0], wa=wa, wb=wb, wo=wo, g_ffn=g_f, w_router_t=w_rt, b_router_t=b_rt)
    x1_p, h2_p, comb_p, pos_p, cnt_p = post(sb_p, df_p, gates_p, xp, cnt0=jnp.zeros((N_EXPERTS, 1), F32),
                                            tm=_tile(n_p, 512))
    x1_s, h2_s, comb_s, pos_s, cnt_all = post(sb_s, df_s, gates_s, xs, cnt0=cnt_p, tm=_tile(n_s, 512))
    h2 = jnp.concatenate([h2_p, h2_s], axis=0)
    comb_t = jnp.concatenate([comb_p, comb_s], axis=1)
    pos_t = jnp.concatenate([pos_p, pos_s], axis=1)
    n_all = n_p + n_s
    tile_meta, n_row_tiles, slab_start, n_chunks, off_row = _routing_tables(pos_t, cnt_all[:, 0], n_all)
    by_window = lambda a: a.reshape(N_EXPERTS, n_all // TOKEN_TILE, TOKEN_TILE)
    rows = _expert_rows(tile_meta, h2, by_window(pos_t), by_window(comb_t), wi, bi, wo_e, bo, n_row_tiles)
    y_p = _combine(slab_start, n_chunks, pos_t, off_row, x1_p, g_fin, rows, 0)
    y_s = _combine(slab_start, n_chunks, pos_t, off_row, x1_s, g_fin, rows, n_p // TOKEN_TILE)
    y_p = y_p.reshape(batch, seq, D_MODEL)
    y_s = y_s.reshape(nb, dq, D_MODEL)

    def from_t(a, feat_shape):
        a = a.reshape((batch,) + feat_shape + (seq,))
        return jnp.moveaxis(a, -1, 1)[None]

    s_lead = (depth, nb, dq)
    return (y_p, y_s,
            from_t(skt, (SB_HEADS, HEAD_DIM)), from_t(svt, (SB_HEADS, HEAD_DIM)),
            from_t(dkt, (DF_HEADS, 2, HEAD_DIM)), dv_p.reshape(depth, batch, seq, DF_HEADS, 2 * HEAD_DIM),
            sk_s.reshape(s_lead + (SB_HEADS, HEAD_DIM)), sv_s.reshape(s_lead + (SB_HEADS, HEAD_DIM)),
            dk_s.reshape(s_lead + (DF_HEADS, 2, HEAD_DIM)), dv_s.reshape(s_lead + (DF_HEADS, 2 * HEAD_DIM)))
```

```python
import functools
import math

import numpy as np
import jax
import jax.numpy as jnp
from jax import lax
from jax.experimental import pallas as pl
from jax.experimental.pallas import tpu as pltpu

F32 = jnp.float32
BF16 = jnp.bfloat16

D_MODEL = 1024
SB_HEADS = 8
DF_HEADS = 4
HEAD_DIM = 64
BRANCH_W = 512
LANES = 128
N_BUCKETS = 32
MAX_DISTANCE = 128
N_EXPERTS = 32
TOP_K = 4
D_FF = 1024
SWIGLU_LIMIT = 7.0
SWIGLU_ALPHA = 1.702
RMS_EPS = 1e-5
QK_SCALE = HEAD_DIM ** -0.5
LOG2E = math.log2(math.e)

BLK = BRANCH_W // LANES
COL_SQ, COL_SK, COL_SV, COL_DK = 0, BLK, 2 * BLK, 3 * BLK
SAMPLE_QKV_W = 6 * BRANCH_W
PROMPT_QKV_W = 4 * BRANCH_W

SB_STOP = 110.0
NEG_BIG = -1e30
VMEM_LIMIT = 56 * 1024 * 1024
PAGES_PER_STEP = 4


def _nt_dot(a, b):
    return lax.dot_general(a, b, (((1,), (1,)), ((), ())), preferred_element_type=F32)


def _dot(a, b):
    return jnp.dot(a, b, preferred_element_type=F32)


def _split_dot(x, w_bf16):
    hi = x.astype(BF16)
    lo = (x - hi.astype(F32)).astype(BF16)
    return _dot(hi, w_bf16) + _dot(lo, w_bf16)


def _softplus(z):
    return jnp.maximum(z, 0.0) + jnp.log1p(jnp.exp(-jnp.abs(z)))


def _rms(x, g):
    return x * lax.rsqrt(jnp.mean(x * x, axis=-1, keepdims=True) + RMS_EPS) * g


def _bias_from_buckets(bucket, value_of):
    out = jnp.zeros(bucket.shape, F32)
    for b in range(N_BUCKETS):
        out = jnp.where(bucket == b, value_of(b), out)
    return out


def _inproj_sample_kernel(x_ref, g_ref, w_ref, qkv_ref, sk_ref, sv_ref, dk_ref, dv_ref, gate_ref):
    h = _rms(x_ref[...], g_ref[...]).astype(BF16)

    def sec(c):
        return _dot(h, w_ref[:, c * BRANCH_W:(c + 1) * BRANCH_W])

    for c, (scale, f32_out) in enumerate(((QK_SCALE, None), (1.0, sk_ref), (1.0, sv_ref),
                                          (QK_SCALE, None), (1.0, dk_ref), (1.0, dv_ref))):
        p = sec(c)
        if f32_out is not None:
            f32_out[...] = p
        qkv_ref[:, c * BRANCH_W:(c + 1) * BRANCH_W] = (p * scale).astype(BF16)
    for c in range(4):
        gate_ref[:, c * BRANCH_W:(c + 1) * BRANCH_W] = sec(6 + c)


def _inproj_sample(x, g, w_bf16, tm):
    n = x.shape[0]
    in_w = w_bf16.shape[1]
    kv = jax.ShapeDtypeStruct((n, BRANCH_W), F32)
    row = lambda i: (i, 0)
    return pl.pallas_call(
        _inproj_sample_kernel,
        grid=(n // tm,),
        in_specs=[pl.BlockSpec((tm, D_MODEL), row),
                  pl.BlockSpec((1, D_MODEL), lambda i: (0, 0)),
                  pl.BlockSpec((D_MODEL, in_w), lambda i: (0, 0))],
        out_specs=[pl.BlockSpec((tm, SAMPLE_QKV_W), row)] + [pl.BlockSpec((tm, BRANCH_W), row)] * 4
                  + [pl.BlockSpec((tm, 2 * D_MODEL), row)],
        out_shape=[jax.ShapeDtypeStruct((n, SAMPLE_QKV_W), BF16), kv, kv, kv, kv,
                   jax.ShapeDtypeStruct((n, 2 * D_MODEL), F32)],
        compiler_params=pltpu.CompilerParams(dimension_semantics=("arbitrary",),
                                             vmem_limit_bytes=VMEM_LIMIT),
        name="inproj_sample",
    )(x, g, w_bf16)


def _inproj_prompt_kernel(x_ref, g_ref, w_ref, wt_ref, qkv_ref, dv_ref, gate_ref,
                          skt_ref, svt_ref, dkt_ref, dqt_ref, dvt_ref):
    h = _rms(x_ref[...], g_ref[...]).astype(BF16)

    def sec(c):
        return _dot(h, w_ref[:, c * BRANCH_W:(c + 1) * BRANCH_W])

    def sec_t(c):
        return _nt_dot(wt_ref[c * BRANCH_W:(c + 1) * BRANCH_W, :], h)

    qkv_ref[:, 0 * BRANCH_W:1 * BRANCH_W] = (sec(0) * QK_SCALE).astype(BF16)
    for c in (1, 2, 3):
        qkv_ref[:, c * BRANCH_W:(c + 1) * BRANCH_W] = sec(c).astype(BF16)
    dv_ref[...] = sec(4)
    for c in range(4):
        gate_ref[:, c * BRANCH_W:(c + 1) * BRANCH_W] = sec(5 + c)
    skt_ref[0] = sec_t(0)
    svt_ref[0] = sec_t(1)
    dkt_ref[0] = sec_t(2)
    dqt_ref[...] = (sec_t(3) * (QK_SCALE * LOG2E)).astype(BF16)
    dvt_ref[0] = sec_t(4).astype(BF16)


def _inproj_prompt(x, g, w_bf16, wt_bf16, batch, seq, tm):
    n = x.shape[0]
    per_seq = seq // tm
    row = lambda i: (i, 0)
    const = lambda i: (0, 0)
    kvt = jax.ShapeDtypeStruct((batch, BRANCH_W, seq), F32)
    kvt_spec = pl.BlockSpec((1, BRANCH_W, tm), lambda i: (i // per_seq, 0, i % per_seq))
    return pl.pallas_call(
        _inproj_prompt_kernel,
        grid=(n // tm,),
        in_specs=[pl.BlockSpec((tm, D_MODEL), row), pl.BlockSpec((1, D_MODEL), const),
                  pl.BlockSpec(w_bf16.shape, const), pl.BlockSpec(wt_bf16.shape, const)],
        out_specs=[pl.BlockSpec((tm, PROMPT_QKV_W), row), pl.BlockSpec((tm, BRANCH_W), row),
                   pl.BlockSpec((tm, 2 * D_MODEL), row), kvt_spec, kvt_spec, kvt_spec,
                   pl.BlockSpec((BRANCH_W, tm), lambda i: (0, i)),
                   pl.BlockSpec((1, BRANCH_W, tm), lambda i: (i, 0, 0))],
        out_shape=[jax.ShapeDtypeStruct((n, PROMPT_QKV_W), BF16), jax.ShapeDtypeStruct((n, BRANCH_W), F32),
                   jax.ShapeDtypeStruct((n, 2 * D_MODEL), F32), kvt, kvt, kvt,
                   jax.ShapeDtypeStruct((BRANCH_W, n), BF16),
                   jax.ShapeDtypeStruct((n // tm, BRANCH_W, tm), BF16)],
        compiler_params=pltpu.CompilerParams(dimension_semantics=("arbitrary",),
                                             vmem_limit_bytes=VMEM_LIMIT),
        name="inproj_prompt",
    )(x, g, w_bf16, wt_bf16)


def _half_masks():
    lane = lax.broadcasted_iota(jnp.int32, (1, LANES), 1)
    return lane < HEAD_DIM, lane >= HEAD_DIM


def _stack_halves(q):
    lo, hi = _half_masks()
    zero = jnp.zeros_like(q)
    return jnp.concatenate([jnp.where(lo, q, zero), jnp.where(hi, q, zero)], axis=0)


def _strict_upper_ones(n):
    j = lax.broadcasted_iota(jnp.int32, (n, n), 0)
    s = lax.broadcasted_iota(jnp.int32, (n, n), 1)
    return jnp.where(j > s, 1.0, 0.0).astype(BF16)


def _sb_scores(z, mask, tri):
    sp = _softplus(z)
    log_keep = -sp if mask is None else jnp.where(mask, -sp, 0.0)
    return log_keep, z - sp + _split_dot(log_keep, tri)


def _sb_attn_kernel(q_ref, k_ref, v_ref, o_ref, acc_ref, car_ref, *, tq):
    i = pl.program_id(2)
    q2 = _stack_halves(q_ref[...])
    tri = _strict_upper_ones(tq)
    row = lax.broadcasted_iota(jnp.int32, (2 * tq, tq), 0)
    row = jnp.where(row >= tq, row - tq, row)
    col = lax.broadcasted_iota(jnp.int32, (2 * tq, tq), 1)
    acc_ref[...] = jnp.zeros_like(acc_ref)
    car_ref[...] = jnp.zeros_like(car_ref)

    def body(state):
        j, _ = state
        ks = pl.multiple_of(j * tq, tq)
        mask = (col + (j - i) * tq) < row
        log_keep, logit = _sb_scores(_nt_dot(q2, k_ref[pl.ds(ks, tq), :]), mask, tri)
        carry = car_ref[...]
        a = jnp.where(mask, jnp.exp(logit + carry), 0.0)
        acc_ref[...] += _dot(a.astype(BF16), v_ref[pl.ds(ks, tq), :])
        carry = carry + jnp.sum(log_keep, axis=1, keepdims=True)
        car_ref[...] = carry
        return j - 1, jnp.max(carry)

    lax.while_loop(lambda s: (s[0] >= 0) & (s[1] > -SB_STOP), body, (i, jnp.float32(0.0)))
    lo, _ = _half_masks()
    o_ref[...] = jnp.where(lo, acc_ref[:tq, :], acc_ref[tq:, :]).astype(o_ref.dtype)


def _sb_attn(qkv, batch, seq, tq):
    nq = seq // tq
    return pl.pallas_call(
        functools.partial(_sb_attn_kernel, tq=tq),
        grid=(batch, BLK, nq),
        in_specs=[pl.BlockSpec((tq, LANES), lambda b, p, i: (b * nq + i, COL_SQ + p)),
                  pl.BlockSpec((seq, LANES), lambda b, p, i: (b, COL_SK + p)),
                  pl.BlockSpec((seq, LANES), lambda b, p, i: (b, COL_SV + p))],
        out_specs=pl.BlockSpec((tq, LANES), lambda b, p, i: (b * nq + i, p)),
        out_shape=jax.ShapeDtypeStruct((batch * seq, BRANCH_W), BF16),
        scratch_shapes=[pltpu.VMEM((2 * tq, LANES), F32), pltpu.VMEM((2 * tq, 1), F32)],
        compiler_params=pltpu.CompilerParams(dimension_semantics=("arbitrary",) * 3,
                                             vmem_limit_bytes=VMEM_LIMIT),
        name="sb_attn",
    )(qkv, qkv, qkv)


def _lambda(lamv_ref, lambda_init):
    a = jnp.sum(lamv_ref[0:1, :] * lamv_ref[1:2, :], axis=1, keepdims=True)
    b = jnp.sum(lamv_ref[2:3, :] * lamv_ref[3:4, :], axis=1, keepdims=True)
    return jnp.exp(a) - jnp.exp(b) + lambda_init


def _df_attn_kernel(tab_ref, lamv_ref, qt_ref, k_ref, vt_ref, bkt_ref, g_ref, o_ref,
                    bias_s, m_s, l_s, acc_s, *, tq, lambda_init):
    h = pl.program_id(1)
    i = pl.program_id(2)

    @pl.when(i == 0)
    def _():
        far = tab_ref[N_BUCKETS - 1, h]
        for t in range(2):
            bias_s[t] = _bias_from_buckets(bkt_ref[t], lambda b: (tab_ref[b, h] - far) * LOG2E)

    qt = qt_ref[...].astype(F32)
    feat = lax.broadcasted_iota(jnp.int32, qt.shape, 0)
    q2t = jnp.concatenate([jnp.where(feat < HEAD_DIM, qt, 0.0), jnp.where(feat >= HEAD_DIM, qt, 0.0)],
                          axis=1).astype(BF16)
    m_s[...] = jnp.full_like(m_s, NEG_BIG)
    l_s[...] = jnp.zeros_like(l_s)
    acc_s[...] = jnp.zeros_like(acc_s)

    def step(blocks):
        scores = []
        for j, bias, mask in blocks:
            s = _dot(k_ref[pl.ds(pl.multiple_of(j * tq, tq), tq), :], q2t)
            if bias is not None:
                s = s + jnp.concatenate([bias, bias], axis=1)
            if mask is not None:
                s = jnp.where(mask, s, NEG_BIG)
            scores.append(s)
        m_old = m_s[...]
        m_new = m_old
        for s in scores:
            m_new = jnp.maximum(m_new, jnp.max(s, axis=0, keepdims=True))
        alpha = jnp.exp2(m_old - m_new)
        l_new = alpha * l_s[...]
        acc = alpha * acc_s[...]
        for (j, _, _), s in zip(blocks, scores):
            p = jnp.exp2(s - m_new)
            l_new = l_new + jnp.sum(p, axis=0, keepdims=True)
            acc = acc + _dot(vt_ref[j], p.astype(BF16))
        l_s[...] = l_new
        acc_s[...] = acc
        m_s[...] = m_new

    n_far = jnp.maximum(i - 1, 0)

    def far_quad(jj, c):
        step([(4 * jj + k, None, None) for k in range(4)])
        return c

    lax.fori_loop(0, n_far // 4, far_quad, 0)
    done = (n_far // 4) * 4

    @pl.when(n_far % 4 >= 2)
    def _():
        step([(done, None, None), (done + 1, None, None)])

    @pl.when(n_far % 2 == 1)
    def _():
        step([(n_far - 1, None, None)])

    key = lax.broadcasted_iota(jnp.int32, (tq, 2 * tq), 0)
    qry = lax.broadcasted_iota(jnp.int32, (tq, 2 * tq), 1)
    qry = jnp.where(qry >= tq, qry - tq, qry)
    diag = (i, bias_s[0], key <= qry)

    @pl.when(i >= 1)
    def _():
        step([(i - 1, bias_s[1], None), diag])

    @pl.when(i == 0)
    def _():
        step([diag])

    lam = _lambda(lamv_ref, lambda_init)
    norm = acc_s[...] / l_s[...]
    out = (norm[:, :tq] - lam * norm[:, tq:]).T
    o_ref[...] = (_rms(out, g_ref[...]) * (1.0 - lambda_init)).astype(o_ref.dtype)


def _df_attn(tab, lamv, dqt, qkv, dvt, buckets, g_subln, batch, seq, tq, lambda_init):
    nq = seq // tq
    return pl.pallas_call(
        functools.partial(_df_attn_kernel, tq=tq, lambda_init=lambda_init),
        grid=(batch, DF_HEADS, nq),
        in_specs=[pl.BlockSpec(memory_space=pltpu.SMEM),
                  pl.BlockSpec((4, HEAD_DIM), lambda b, h, i: (0, 0)),
                  pl.BlockSpec((LANES, tq), lambda b, h, i: (h, b * nq + i)),
                  pl.BlockSpec((seq, LANES), lambda b, h, i: (b, COL_DK + h)),
                  pl.BlockSpec((nq, LANES, tq), lambda b, h, i: (b, h, 0)),
                  pl.BlockSpec((2, tq, tq), lambda b, h, i: (0, 0, 0)),
                  pl.BlockSpec((1, LANES), lambda b, h, i: (0, 0))],
        out_specs=pl.BlockSpec((tq, LANES), lambda b, h, i: (b * nq + i, h)),
        out_shape=jax.ShapeDtypeStruct((batch * seq, BRANCH_W), BF16),
        scratch_shapes=[pltpu.VMEM((2, tq, tq), F32), pltpu.VMEM((1, 2 * tq), F32),
                        pltpu.VMEM((1, 2 * tq), F32), pltpu.VMEM((LANES, 2 * tq), F32)],
        compiler_params=pltpu.CompilerParams(dimension_semantics=("arbitrary",) * 3,
                                             vmem_limit_bytes=VMEM_LIMIT),
        name="df_attn",
    )(tab, lamv, dqt, qkv, dvt, buckets, g_subln)


def _decode_kernel(pt_ref, tab_ref, lamv_ref, qkv_ref, *rest, dq, page, lambda_init, group):
    ksb_hbm, vsb_hbm = rest[:2]
    kdf, vdf = (rest[2 + g * group:2 + (g + 1) * group] for g in range(2))
    (bkt_ref, g_ref, osb_ref, odf_ref, qsb_s, qdf_s, bias_s, acc_sb, car_s, acc_df, m_s, l_s,
     ksb_buf, vsb_buf, sb_sem, sb_live) = rest[2 + 2 * group:]
    b = pl.program_id(0)
    s = pl.program_id(1)
    n_steps = pl.num_programs(1)
    n_pages = (n_steps - 1) * group
    rows = SB_HEADS * dq
    hrows = rows // DF_HEADS
    tri = _strict_upper_ones(page)

    def sb_copies(step, slot):
        out = []
        for g in range(group):
            pg = pt_ref[b, n_pages - 1 - ((step - 1) * group + g)]
            src = pl.ds(pl.multiple_of(pg * BRANCH_W, BRANCH_W), BRANCH_W)
            out.append(pltpu.make_async_copy(ksb_hbm.at[src, :], ksb_buf.at[slot, g], sb_sem.at[slot]))
            out.append(pltpu.make_async_copy(vsb_hbm.at[src, :], vsb_buf.at[slot, g], sb_sem.at[slot]))
        return out

    def sb_prefetch(step):
        go = (step < n_steps) & (jnp.max(car_s[...]) > -SB_STOP)
        sb_live[0] = go.astype(jnp.int32)

        @pl.when(go)
        def _():
            for cp in sb_copies(step, step % 2):
                cp.start()

    def sb_update(z_list, mask, v_list, v_transposed):
        carry = car_s[...]
        acc = acc_sb[...]
        for z, v in zip(z_list, v_list):
            log_keep, logit = _sb_scores(z, mask, tri)
            a = jnp.exp(logit + carry)
            if mask is not None:
                a = jnp.where(mask, a, 0.0)
            a = a.astype(BF16)
            acc = acc + (_nt_dot(a, v) if v_transposed else _dot(a, v))
            carry = carry + jnp.sum(log_keep, axis=1, keepdims=True)
        acc_sb[...] = acc
        car_s[...] = carry

    def df_update(s_list, v_of):
        m_old = m_s[...]
        m_new = m_old
        for sc in s_list:
            m_new = jnp.maximum(m_new, jnp.max(sc, axis=1, keepdims=True))
        alpha = jnp.exp(m_old - m_new)
        p_list = [jnp.exp(sc - m_new) for sc in s_list]
        l_new = alpha * l_s[...]
        for p in p_list:
            l_new = l_new + jnp.sum(p, axis=1, keepdims=True)
        for h in range(DF_HEADS):
            sl = slice(h * hrows, (h + 1) * hrows)
            acc = alpha[sl] * acc_df[sl, :]
            for g, p in enumerate(p_list):
                acc = acc + _dot(p[sl].astype(BF16), v_of(g, h))
            acc_df[sl, :] = acc
        m_s[...] = m_new
        l_s[...] = l_new

    @pl.when(s == 0)
    def _():
        rq = lax.broadcasted_iota(jnp.int32, (rows, BRANCH_W), 0) // dq
        lq = lax.broadcasted_iota(jnp.int32, (rows, BRANCH_W), 1) // HEAD_DIM
        for c, dst in ((0, qsb_s), (3, qdf_s)):
            q = qkv_ref[0, :, c * BRANCH_W:(c + 1) * BRANCH_W].astype(F32)
            qt = jnp.concatenate([q] * SB_HEADS, axis=0)
            dst[...] = jnp.where(rq == lq, qt, 0.0).astype(BF16)
        for h in range(DF_HEADS):
            far = tab_ref[N_BUCKETS - 1, h]
            for t in range(2):
                bias_s[t, h * hrows:(h + 1) * hrows, :] = _bias_from_buckets(
                    bkt_ref[t, h * hrows:(h + 1) * hrows, :], lambda b: tab_ref[b, h] - far)
        acc_sb[...] = jnp.zeros_like(acc_sb)
        car_s[...] = jnp.zeros_like(car_s)
        acc_df[...] = jnp.zeros_like(acc_df)
        m_s[...] = jnp.full_like(m_s, NEG_BIG)
        l_s[...] = jnp.zeros_like(l_s)

        def new_rows(c):
            x = qkv_ref[0, :, c * BRANCH_W:(c + 1) * BRANCH_W].astype(F32)
            return jnp.concatenate([x, jnp.zeros((page - dq, BRANCH_W), F32)], axis=0).astype(BF16)

        row = lax.broadcasted_iota(jnp.int32, (rows, page), 0) % dq
        col = lax.broadcasted_iota(jnp.int32, (rows, page), 1)
        sb_update([_nt_dot(qsb_s[...], new_rows(1))], col < row, [new_rows(2)], False)
        v_new = new_rows(5)
        sc = jnp.where(col <= row, _nt_dot(qdf_s[...], new_rows(4)) + bias_s[0], NEG_BIG)
        df_update([sc], lambda g, h: v_new[:, h * LANES:(h + 1) * LANES])
        sb_prefetch(1)

    def cached_step(newest):
        @pl.when(sb_live[0] == 1)
        def _():
            slot = s % 2
            for cp in sb_copies(s, slot):
                cp.wait()
            q = qsb_s[...]
            sb_update([_dot(q, ksb_buf[slot, g].astype(BF16)) for g in range(group)], None,
                      [vsb_buf[slot, g].astype(BF16) for g in range(group)], True)
            sb_prefetch(s + 1)

        q = qdf_s[...]
        scores = [_dot(q, k[...].astype(BF16)) for k in kdf]
        if newest:
            scores[0] = scores[0] + bias_s[1]
        df_update(scores, lambda g, h: vdf[g][pl.ds(h, page, stride=DF_HEADS), :].astype(BF16))

    @pl.when(s == 1)
    def _():
        cached_step(True)

    @pl.when(s >= 2)
    def _():
        cached_step(False)

    @pl.when(s == n_steps - 1)
    def _():
        rq = lax.broadcasted_iota(jnp.int32, (rows, BRANCH_W), 0)
        lq = lax.broadcasted_iota(jnp.int32, (rows, BRANCH_W), 1)
        sb = jnp.where(rq // dq == lq // HEAD_DIM, acc_sb[...], 0.0)
        sb = functools.reduce(lambda a, b: a + b, [sb[g * dq:(g + 1) * dq, :] for g in range(SB_HEADS)])
        osb_ref[0] = sb.astype(osb_ref.dtype)

        lam = _lambda(lamv_ref, lambda_init)
        norm = acc_df[...] / l_s[...]
        g = g_ref[...]
        heads = []
        for h in range(DF_HEADS):
            o = norm[h * hrows:h * hrows + dq, :] - lam * norm[h * hrows + dq:(h + 1) * hrows, :]
            heads.append(_rms(o, g))
        odf_ref[0] = (jnp.concatenate(heads, axis=1) * (1.0 - lambda_init)).astype(odf_ref.dtype)


def _decode(page_table, tab, lamv, qkv3, caches, dec_buckets, g_subln, lambda_init, page):
    nb, dq, _ = qkv3.shape
    n_pages = page_table.shape[1]
    group = PAGES_PER_STEP
    while n_pages % group:
        group //= 2
    rows = SB_HEADS * dq

    def cache_spec(g):
        def index(b, s, pt):
            return (pt[b, n_pages - 1 - (jnp.maximum(s - 1, 0) * group + g)], 0)
        return pl.BlockSpec((BRANCH_W, page), index)

    cache_specs = [pl.BlockSpec(memory_space=pl.ANY)] * 2 + [cache_spec(g) for _ in range(2) for g in range(group)]
    cache_args = list(caches[:2]) + [c for c in caches[2:] for _ in range(group)]
    out = jax.ShapeDtypeStruct((nb, dq, BRANCH_W), BF16)
    const2 = lambda b, s, pt: (0, 0)
    grid_spec = pltpu.PrefetchScalarGridSpec(
        num_scalar_prefetch=1,
        grid=(nb, n_pages // group + 1),
        in_specs=[pl.BlockSpec(memory_space=pltpu.SMEM),
                  pl.BlockSpec((4, HEAD_DIM), const2),
                  pl.BlockSpec((1, dq, SAMPLE_QKV_W), lambda b, s, pt: (b, 0, 0))]
                 + cache_specs
                 + [pl.BlockSpec((2, rows, page), lambda b, s, pt: (0, 0, 0)),
                    pl.BlockSpec((1, LANES), const2)],
        out_specs=[pl.BlockSpec((1, dq, BRANCH_W), lambda b, s, pt: (b, 0, 0))] * 2,
        scratch_shapes=[pltpu.VMEM((rows, BRANCH_W), BF16), pltpu.VMEM((rows, BRANCH_W), BF16),
                        pltpu.VMEM((2, rows, page), F32),
                        pltpu.VMEM((rows, BRANCH_W), F32), pltpu.VMEM((rows, 1), F32),
                        pltpu.VMEM((rows, LANES), F32), pltpu.VMEM((rows, 1), F32),
                        pltpu.VMEM((rows, 1), F32),
                        pltpu.VMEM((2, group, BRANCH_W, page), F32), pltpu.VMEM((2, group, BRANCH_W, page), F32),
                        pltpu.SemaphoreType.DMA((2,)), pltpu.SMEM((1,), jnp.int32)],
    )
    return pl.pallas_call(
        functools.partial(_decode_kernel, dq=dq, page=page, lambda_init=lambda_init, group=group),
        grid_spec=grid_spec,
        out_shape=[out, out],
        compiler_params=pltpu.CompilerParams(dimension_semantics=("arbitrary", "arbitrary"),
                                             vmem_limit_bytes=VMEM_LIMIT),
        name="decode_attn",
    )(page_table, tab, lamv, qkv3, *cache_args, dec_buckets, g_subln)


def _post_kernel(sb_ref, df_ref, ga_ref, gb_ref, x_ref, bg_ref, wa_ref, wb_ref, wo_ref, gf_ref,
                 wrt_ref, brt_ref, cnt0_ref, x1_ref, h2_ref, comb_ref, pos_ref, cnt_ref, run_s):
    @pl.when(pl.program_id(0) == 0)
    def _():
        run_s[...] = cnt0_ref[...]

    a = _dot(sb_ref[...], wa_ref[...])
    b = _dot(df_ref[...], wb_ref[...])
    merged = jax.nn.sigmoid(ga_ref[...] + bg_ref[0:1, :]) * a + jax.nn.sigmoid(gb_ref[...] + bg_ref[1:2, :]) * b
    x1 = x_ref[...] + _dot(merged.astype(BF16), wo_ref[...])
    x1_ref[...] = x1
    h2 = _rms(x1, gf_ref[...])
    h2_ref[...] = h2.astype(BF16)

    wr = wrt_ref[...]
    wr_hi = wr.astype(BF16)
    wr_lo = (wr - wr_hi.astype(F32)).astype(BF16)
    h_hi = h2.astype(BF16)
    h_lo = (h2 - h_hi.astype(F32)).astype(BF16)
    logits = _nt_dot(wr_hi, h_hi) + _nt_dot(wr_lo, h_hi) + _nt_dot(wr_hi, h_lo) + brt_ref[...]

    expert = lax.broadcasted_iota(jnp.int32, logits.shape, 0).astype(F32)
    vals, hots = [], []
    for _ in range(TOP_K):
        mx = jnp.max(logits, axis=0, keepdims=True)
        first = jnp.min(jnp.where(logits == mx, expert, float(N_EXPERTS)), axis=0, keepdims=True)
        hot = expert == first
        vals.append(mx)
        hots.append(hot)
        logits = jnp.where(hot, -jnp.inf, logits)
    es = [jnp.exp(v - vals[0]) for v in vals]
    denom = functools.reduce(lambda u, v: u + v, es)
    comb = jnp.zeros(logits.shape, F32)
    routed = jnp.zeros(logits.shape, F32)
    for e, hot in zip(es, hots):
        comb = comb + jnp.where(hot, e / denom, 0.0)
        routed = routed + jnp.where(hot, 1.0, 0.0)
    comb_ref[...] = comb

    tm = logits.shape[1]
    j = lax.broadcasted_iota(jnp.int32, (tm, tm), 0)
    s = lax.broadcasted_iota(jnp.int32, (tm, tm), 1)
    earlier = jnp.where(j < s, 1.0, 0.0).astype(BF16)
    rank = _dot(routed.astype(BF16), earlier) + run_s[...]
    pos_ref[...] = jnp.where(routed > 0.0, rank, -1.0 - rank)
    run = run_s[...] + jnp.sum(routed, axis=1, keepdims=True)
    run_s[...] = run
    cnt_ref[...] = run


def _post(sb, df, gates, x, b_gate, wa, wb, wo, g_ffn, w_router_t, b_router_t, cnt0, tm):
    n = x.shape[0]
    row = lambda i: (i, 0)
    col = lambda i: (0, i)
    const = lambda i: (0, 0)
    routing = jax.ShapeDtypeStruct((N_EXPERTS, n), F32)
    return pl.pallas_call(
        _post_kernel,
        grid=(n // tm,),
        in_specs=[pl.BlockSpec((tm, BRANCH_W), row), pl.BlockSpec((tm, BRANCH_W), row),
                  pl.BlockSpec((tm, D_MODEL), lambda i: (i, 0)), pl.BlockSpec((tm, D_MODEL), lambda i: (i, 1)),
                  pl.BlockSpec((tm, D_MODEL), row),
                  pl.BlockSpec((2, D_MODEL), const),
                  pl.BlockSpec((BRANCH_W, D_MODEL), const), pl.BlockSpec((BRANCH_W, D_MODEL), const),
                  pl.BlockSpec((D_MODEL, D_MODEL), const),
                  pl.BlockSpec((1, D_MODEL), const),
                  pl.BlockSpec((N_EXPERTS, D_MODEL), const), pl.BlockSpec((N_EXPERTS, 1), const),
                  pl.BlockSpec((N_EXPERTS, 1), const)],
        out_specs=[pl.BlockSpec((tm, D_MODEL), row), pl.BlockSpec((tm, D_MODEL), row),
                   pl.BlockSpec((N_EXPERTS, tm), col), pl.BlockSpec((N_EXPERTS, tm), col),
                   pl.BlockSpec((N_EXPERTS, 1), const)],
        out_shape=[jax.ShapeDtypeStruct((n, D_MODEL), F32), jax.ShapeDtypeStruct((n, D_MODEL), BF16),
                   routing, routing, jax.ShapeDtypeStruct((N_EXPERTS, 1), F32)],
        scratch_shapes=[pltpu.VMEM((N_EXPERTS, 1), F32)],
        compiler_params=pltpu.CompilerParams(dimension_semantics=("arbitrary",),
                                             vmem_limit_bytes=VMEM_LIMIT),
        name="post_attn",
    )(sb, df, gates, gates, x, b_gate, wa, wb, wo, g_ffn, w_router_t, b_router_t, cnt0)


ROW_TILE = 256
TOKEN_TILE = 256
SLAB = 64
GATHER_ROWS = 128
GATHER_SPAN = 6
SLAB_ALIGN = 16


def _expert_rows_kernel(te_ref, q0_ref, wlo_ref, whi_ref, valid_ref,
                        h_ref, pos_ref, comb_ref, wi_ref, bi_ref, wo_ref, bo_ref, o_ref, x_s, g_s, *, span):
    r = pl.program_id(0)

    @pl.when(valid_ref[r] == 0)
    def _():
        o_ref[...] = jnp.zeros_like(o_ref)

    @pl.when(valid_ref[r] == 1)
    def _():
        e = te_ref[r]
        q0 = q0_ref[r].astype(F32)
        n_win = pos_ref.shape[1]
        parts = ROW_TILE // GATHER_ROWS
        lo = [wlo_ref[r * parts + p] for p in range(parts)]
        hi = [whi_ref[r * parts + p] for p in range(parts)]
        fits = functools.reduce(lambda a, b: a & b, [hi[p] - lo[p] <= span for p in range(parts)])

        def picks(rows, row0, w0, k):
            row = lax.broadcasted_iota(jnp.int32, (rows, TOKEN_TILE), 0).astype(F32) + (q0 + row0)
            hits = [pos_ref[e, pl.ds(w0 + i, 1), :] == row for i in range(k)]
            gsum = functools.reduce(lambda a, b: a + b,
                                    [jnp.where(h, comb_ref[e, pl.ds(w0 + i, 1), :], 0.0) for i, h in enumerate(hits)])
            return [jnp.where(h, 1.0, 0.0).astype(BF16) for h in hits], gsum

        @pl.when(fits)
        def _():
            for p in range(parts):
                rows = slice(p * GATHER_ROWS, (p + 1) * GATHER_ROWS)
                w0 = jnp.minimum(lo[p], n_win - span)
                onehot, gsum = picks(GATHER_ROWS, p * GATHER_ROWS, w0, span)
                tok0 = pl.multiple_of(w0 * TOKEN_TILE, TOKEN_TILE)
                x_s[rows, :] = _dot(jnp.concatenate(onehot, axis=1), h_ref[pl.ds(tok0, span * TOKEN_TILE), :])
                g_s[rows, :] = gsum

        @pl.when(jnp.logical_not(fits))
        def _():
            x_s[...] = jnp.zeros_like(x_s)
            g_s[...] = jnp.zeros_like(g_s)

            def window(w, c):
                onehot, gsum = picks(ROW_TILE, 0, w, 1)
                x_s[...] += _dot(onehot[0], h_ref[pl.ds(pl.multiple_of(w * TOKEN_TILE, TOKEN_TILE), TOKEN_TILE), :])
                g_s[...] += gsum
                return c

            lax.fori_loop(lo[0], hi[parts - 1], window, 0)

        h = x_s[...].astype(BF16)
        gate = jnp.minimum(_dot(h, wi_ref[0, :, :D_FF]) + bi_ref[0, :, :D_FF], SWIGLU_LIMIT)
        up = jnp.clip(_dot(h, wi_ref[0, :, D_FF:]) + bi_ref[0, :, D_FF:], -SWIGLU_LIMIT, SWIGLU_LIMIT)
        act = (up + 1.0) * gate * jax.nn.sigmoid(SWIGLU_ALPHA * gate)
        weight = jnp.sum(g_s[...], axis=1, keepdims=True)
        o_ref[...] = (weight * (_dot(act.astype(BF16), wo_ref[0]) + bo_ref[0])).astype(o_ref.dtype)


def _expert_rows(tile_meta, h2, pos3, comb3, wi, bi, wo, bo, n_row_tiles):
    n = h2.shape[0]
    once = pl.Buffered(1)
    exp3 = lambda r, te, *_: (te[r], 0, 0)
    span = min(GATHER_SPAN, pos3.shape[1])
    grid_spec = pltpu.PrefetchScalarGridSpec(
        num_scalar_prefetch=5,
        grid=(n_row_tiles,),
        in_specs=[pl.BlockSpec((n, D_MODEL), lambda r, *_: (0, 0), pipeline_mode=once),
                  pl.BlockSpec(pos3.shape, lambda r, *_: (0, 0, 0), pipeline_mode=once),
                  pl.BlockSpec(comb3.shape, lambda r, *_: (0, 0, 0), pipeline_mode=once),
                  pl.BlockSpec((1, D_MODEL, 2 * D_FF), exp3, pipeline_mode=once),
                  pl.BlockSpec((1, 1, 2 * D_FF), exp3),
                  pl.BlockSpec((1, D_FF, D_MODEL), exp3, pipeline_mode=once),
                  pl.BlockSpec((1, 1, D_MODEL), exp3)],
        out_specs=pl.BlockSpec((ROW_TILE, D_MODEL), lambda r, *_: (r, 0)),
        scratch_shapes=[pltpu.VMEM((ROW_TILE, D_MODEL), F32), pltpu.VMEM((ROW_TILE, TOKEN_TILE), F32)],
    )
    return pl.pallas_call(
        functools.partial(_expert_rows_kernel, span=span),
        grid_spec=grid_spec,
        out_shape=jax.ShapeDtypeStruct((n_row_tiles * ROW_TILE, D_MODEL), BF16),
        compiler_params=pltpu.CompilerParams(dimension_semantics=("arbitrary",),
                                             vmem_limit_bytes=VMEM_LIMIT),
        name="expert_rows",
    )(*tile_meta, h2, pos3, comb3, wi, bi, wo, bo)


def _combine_kernel(start_ref, nchunk_ref, pos_ref, off_ref, x1_ref, gfin_ref, rows_hbm, y_ref,
                    buf, sem, *, tile0, last_start):
    t = pl.program_id(0) + tile0
    pad = jnp.zeros((LANES - N_EXPERTS, TOKEN_TILE), F32)
    rank = jnp.concatenate([pos_ref[...], pad], axis=0).T
    dest = jnp.where(rank >= 0.0, rank + off_ref[...], -1.0)
    lane = lax.broadcasted_iota(jnp.int32, (1, 2 * SLAB), 1).astype(F32)

    def chunk(c, acc):
        starts = [start_ref[t * N_EXPERTS + e] + c * SLAB for e in range(N_EXPERTS)]
        copies = []
        for e in range(N_EXPERTS):
            src = pl.multiple_of(jnp.minimum(starts[e], last_start), SLAB_ALIGN)
            cp = pltpu.make_async_copy(rows_hbm.at[pl.ds(src, SLAB), :], buf.at[pl.ds(e * SLAB, SLAB), :], sem)
            cp.start()
            copies.append(cp)
        for cp in copies:
            cp.wait()
        pieces = []
        for e in range(0, N_EXPERTS, 2):
            rel = []
            for k in range(2):
                d = dest[:, e + k:e + k + 1] - starts[e + k].astype(F32)
                rel.append(jnp.where((d >= 0.0) & (d < SLAB), d + k * SLAB, -1.0))
            hit = (rel[0] == lane) | (rel[1] == lane)
            pieces.append(jnp.where(hit, 1.0, 0.0).astype(BF16))
        return acc + _dot(jnp.concatenate(pieces, axis=1), buf[...])

    acc = lax.fori_loop(0, nchunk_ref[t], chunk, jnp.zeros((TOKEN_TILE, D_MODEL), F32))
    y_ref[...] = _rms(x1_ref[...] + acc, gfin_ref[...])


def _combine(slab_start, n_chunks, pos_t, off_row, x1, g_final, rows, tile0):
    n = x1.shape[0]
    n_rows = rows.shape[0]
    grid_spec = pltpu.PrefetchScalarGridSpec(
        num_scalar_prefetch=2,
        grid=(n // TOKEN_TILE,),
        in_specs=[pl.BlockSpec((N_EXPERTS, TOKEN_TILE), lambda i, *_: (0, i + tile0)),
                  pl.BlockSpec((1, LANES), lambda i, *_: (0, 0)),
                  pl.BlockSpec((TOKEN_TILE, D_MODEL), lambda i, *_: (i, 0)),
                  pl.BlockSpec((1, D_MODEL), lambda i, *_: (0, 0)),
                  pl.BlockSpec(memory_space=pl.ANY)],
        out_specs=pl.BlockSpec((TOKEN_TILE, D_MODEL), lambda i, *_: (i, 0)),
        scratch_shapes=[pltpu.VMEM((N_EXPERTS * SLAB, D_MODEL), BF16), pltpu.SemaphoreType.DMA(())],
    )
    return pl.pallas_call(
        functools.partial(_combine_kernel, tile0=tile0, last_start=n_rows - SLAB),
        grid_spec=grid_spec,
        out_shape=jax.ShapeDtypeStruct((n, D_MODEL), F32),
        compiler_params=pltpu.CompilerParams(dimension_semantics=("arbitrary",),
                                             vmem_limit_bytes=VMEM_LIMIT),
        name="combine",
    )(slab_start, n_chunks, pos_t, off_row, x1, g_final, rows)


def _routing_tables(pos_t, counts, n_tokens):
    n_win = n_tokens // TOKEN_TILE
    max_tiles = (TOP_K * n_tokens + N_EXPERTS * (ROW_TILE - 1)) // ROW_TILE
    n_row_tiles = max_tiles + 1
    counts = counts.astype(jnp.int32)
    tiles_e = (counts + ROW_TILE - 1) // ROW_TILE
    tile_end = jnp.cumsum(tiles_e)
    tile_start = tile_end - tiles_e
    off = tile_start * ROW_TILE
    r = jnp.arange(n_row_tiles, dtype=jnp.int32)
    te = jnp.minimum(jnp.sum(r[:, None] >= tile_end[None, :], axis=1), N_EXPERTS - 1).astype(jnp.int32)
    valid = (r < tile_end[-1]).astype(jnp.int32)
    mine = (te[:, None] == jnp.arange(N_EXPERTS, dtype=jnp.int32)[None, :]).astype(jnp.int32)
    q0 = (r - jnp.sum(mine * tile_start[None, :], axis=1)) * ROW_TILE
    first = pos_t[:, ::TOKEN_TILE]
    before = jnp.where(first < 0, -1.0 - first, first).astype(jnp.int32)
    edges = jnp.concatenate([before, counts[:, None]], axis=1)
    edges_r = jnp.sum(mine[:, :, None] * edges[None, :, :], axis=1)
    e_lo, e_hi = edges_r[:, :-1], edges_r[:, 1:]
    part_q0 = q0[:, None] + jnp.arange(0, ROW_TILE, GATHER_ROWS, dtype=jnp.int32)[None, :]
    wlo = jnp.sum(e_hi[:, None, :] <= part_q0[:, :, None], axis=2).astype(jnp.int32) * valid[:, None]
    whi = jnp.sum(e_lo[:, None, :] < (part_q0 + GATHER_ROWS)[:, :, None], axis=2).astype(jnp.int32) * valid[:, None]
    wlo, whi = wlo.reshape(-1), whi.reshape(-1)
    start = off[:, None] + before
    start_al = (start // SLAB_ALIGN) * SLAB_ALIGN
    need = start - start_al + (edges[:, 1:] - edges[:, :-1])
    n_chunks = jnp.maximum(jnp.max((need + SLAB - 1) // SLAB, axis=0), 1).astype(jnp.int32)
    slab_start = start_al.T.reshape(-1).astype(jnp.int32)
    off_row = jnp.pad(off.astype(F32), (0, LANES - N_EXPERTS))[None, :]
    return (te, q0.astype(jnp.int32), wlo, whi, valid), n_row_tiles, slab_start, n_chunks, off_row


def _bucket_np(n):
    max_exact = N_BUCKETS // 2
    nf = np.maximum(n, max_exact).astype(np.float32)
    large = max_exact + (np.log(nf / np.float32(max_exact)) / np.float32(math.log(MAX_DISTANCE / max_exact))
                         * np.float32(N_BUCKETS - max_exact)).astype(np.int32)
    return np.where(n < max_exact, n, np.minimum(large, N_BUCKETS - 1)).astype(np.int32)


def _tile(n, cap):
    t = cap
    while n % t:
        t //= 2
    return t


def kernel(x_prompt, x_sample, cache_sb_k, cache_sb_v, cache_df_k, cache_df_v, page_table, rel_bias, g_attn, w_in, b_gate, df_lambda_q1, df_lambda_k1, df_lambda_q2, df_lambda_k2, g_subln, w_proj_a, w_proj_b, w_out, g_ffn, w_router, b_router, w_exp_in, b_exp_in, w_exp_out, b_exp_out, g_final):
    depth = w_in.shape[0]
    assert depth == 1, "single-layer step"
    batch, seq, _ = x_prompt.shape
    nb, dq, _ = x_sample.shape
    n_pool, page = cache_sb_k.shape[1], cache_sb_k.shape[2]
    tq = _tile(seq, 256)
    assert tq >= MAX_DISTANCE and page >= MAX_DISTANCE, "bias is constant beyond the two nearest tiles"
    lambda_init = 0.8 - 0.6 * math.exp(-0.3 * 0)

    w = w_in[0]
    sec = lambda c: w[:, c * BRANCH_W:(c + 1) * BRANCH_W]
    w_in_b = w.astype(BF16)
    w_rows = jnp.concatenate([sec(0), sec(1), sec(2), sec(4), sec(5), w[:, 6 * BRANCH_W:]], axis=1).astype(BF16)
    w_cols = jnp.concatenate([sec(1), sec(2), sec(4), sec(3), sec(5)], axis=1).T.astype(BF16)
    wa, wb, wo = w_proj_a[0].astype(BF16), w_proj_b[0].astype(BF16), w_out[0].astype(BF16)
    wi, wo_e = w_exp_in[0].astype(BF16), w_exp_out[0].astype(BF16)
    bi, bo = b_exp_in[0][:, None, :], b_exp_out[0][:, None, :]
    lamv = jnp.stack([df_lambda_q1[0], df_lambda_k1[0], df_lambda_q2[0], df_lambda_k2[0]]).astype(F32)
    g_sub = g_subln[0][None, :]
    g_a, g_f, g_fin = g_attn[0][None, :], g_ffn[0][None, :], g_final[None, :]
    b_r = b_router[0][None, :]

    k_i = np.arange(tq)[:, None]
    q_i = np.arange(tq)[None, :]
    near_buckets = jnp.asarray(_bucket_np(np.stack([np.maximum(q_i - k_i, 0), q_i - k_i + tq])))
    r_i = (np.arange(SB_HEADS * dq) % dq)[:, None]
    c_i = np.arange(page)[None, :]
    dec_buckets = jnp.asarray(_bucket_np(np.stack([np.maximum(r_i - c_i, 0), r_i + page - c_i])))

    xp = x_prompt.reshape(batch * seq, D_MODEL)
    xs = x_sample.reshape(nb * dq, D_MODEL)

    qkv_p, dv_p, gates_p, skt, svt, dkt, dqt, dvt = _inproj_prompt(xp, g_a, w_rows, w_cols, batch, seq, tq)
    qkv_s, sk_s, sv_s, dk_s, dv_s, gates_s = _inproj_sample(xs, g_a, w_in_b, _tile(nb * dq, 256))

    sb_p = _sb_attn(qkv_p, batch, seq, tq)
    df_p = _df_attn(rel_bias, lamv, dqt, qkv_p, dvt, near_buckets, g_sub, batch, seq, tq, lambda_init)

    rows_of_pool = n_pool * BRANCH_W
    caches = [jnp.transpose(cache_sb_k[0], (0, 2, 3, 1)).reshape(rows_of_pool, page),
              jnp.transpose(cache_sb_v[0], (0, 2, 3, 1)).reshape(rows_of_pool, page),
              jnp.transpose(cache_df_k[0], (0, 2, 3, 4, 1)).reshape(rows_of_pool, page),
              cache_df_v[0].reshape(rows_of_pool, LANES)]
    sb_s, df_s = _decode(page_table, rel_bias, lamv, qkv_s.reshape(nb, dq, SAMPLE_QKV_W), caches, dec_buckets,
                         g_sub, lambda_init, page)
    sb_s = sb_s.reshape(nb * dq, BRANCH_W)
    df_s = df_s.reshape(nb * dq, BRANCH_W)

    n_p, n_s = batch * seq, nb * dq
    assert n_p % TOKEN_TILE == 0 and n_s % TOKEN_TILE == 0
    w_rt, b_rt = w_router[0].T, b_router[0][:, None]
    post = functools.partial(_post, b_gate=b_gate[0], wa=wa, wb=wb, wo=wo, g_ffn=g_f, w_router_t=w_rt, b_router_t=b_rt)
    x1_p, h2_p, comb_p, pos_p, cnt_p = post(sb_p, df_p, gates_p, xp, cnt0=jnp.zeros((N_EXPERTS, 1), F32),
                                            tm=_tile(n_p, 512))
    x1_s, h2_s, comb_s, pos_s, cnt_all = post(sb_s, df_s, gates_s, xs, cnt0=cnt_p, tm=_tile(n_s, 512))
    h2 = jnp.concatenate([h2_p, h2_s], axis=0)
    comb_t = jnp.concatenate([comb_p, comb_s], axis=1)
    pos_t = jnp.concatenate([pos_p, pos_s], axis=1)
    n_all = n_p + n_s
    tile_meta, n_row_tiles, slab_start, n_chunks, off_row = _routing_tables(pos_t, cnt_all[:, 0], n_all)
    by_window = lambda a: a.reshape(N_EXPERTS, n_all // TOKEN_TILE, TOKEN_TILE)
    rows = _expert_rows(tile_meta, h2, by_window(pos_t), by_window(comb_t), wi, bi, wo_e, bo, n_row_tiles)
    y_p = _combine(slab_start, n_chunks, pos_t, off_row, x1_p, g_fin, rows, 0)
    y_s = _combine(slab_start, n_chunks, pos_t, off_row, x1_s, g_fin, rows, n_p // TOKEN_TILE)
    y_p = y_p.reshape(batch, seq, D_MODEL)
    y_s = y_s.reshape(nb, dq, D_MODEL)

    def from_t(a, feat_shape):
        a = a.reshape((batch,) + feat_shape + (seq,))
        return jnp.moveaxis(a, -1, 1)[None]

    s_lead = (depth, nb, dq)
    return (y_p, y_s,
            from_t(skt, (SB_HEADS, HEAD_DIM)), from_t(svt, (SB_HEADS, HEAD_DIM)),
            from_t(dkt, (DF_HEADS, 2, HEAD_DIM)), dv_p.reshape(depth, batch, seq, DF_HEADS, 2 * HEAD_DIM),
            sk_s.reshape(s_lead + (SB_HEADS, HEAD_DIM)), sv_s.reshape(s_lead + (SB_HEADS, HEAD_DIM)),
            dk_s.reshape(s_lead + (DF_HEADS, 2, HEAD_DIM)), dv_s.reshape(s_lead + (DF_HEADS, 2 * HEAD_DIM)))
```

```python
import functools
import math

import numpy as np
import jax
import jax.numpy as jnp
from jax import lax
from jax.experimental import pallas as pl
from jax.experimental.pallas import tpu as pltpu

F32 = jnp.float32
BF16 = jnp.bfloat16

D_MODEL = 1024
SB_HEADS = 8
DF_HEADS = 4
HEAD_DIM = 64
BRANCH_W = 512
LANES = 128
N_BUCKETS = 32
MAX_DISTANCE = 128
N_EXPERTS = 32
TOP_K = 4
D_FF = 1024
SWIGLU_LIMIT = 7.0
SWIGLU_ALPHA = 1.702
RMS_EPS = 1e-5
QK_SCALE = HEAD_DIM ** -0.5
LOG2E = math.log2(math.e)

BLK = BRANCH_W // LANES
COL_SQ, COL_SK, COL_SV, COL_DK = 0, BLK, 2 * BLK, 3 * BLK
SAMPLE_QKV_W = 6 * BRANCH_W
PROMPT_QKV_W = 4 * BRANCH_W

SB_STOP = 110.0
NEG_BIG = -1e30
FAR_GROUP = 4
VMEM_LIMIT = 56 * 1024 * 1024
PAGES_PER_STEP = 8


def _nt_dot(a, b):
    return lax.dot_general(a, b, (((1,), (1,)), ((), ())), preferred_element_type=F32)


def _dot(a, b):
    return jnp.dot(a, b, preferred_element_type=F32)


def _split_dot(x, w_bf16):
    hi = x.astype(BF16)
    lo = (x - hi.astype(F32)).astype(BF16)
    return _dot(hi, w_bf16) + _dot(lo, w_bf16)


def _softplus(z):
    return jnp.maximum(z, 0.0) + jnp.log1p(jnp.exp(-jnp.abs(z)))


def _rms(x, g):
    return x * lax.rsqrt(jnp.mean(x * x, axis=-1, keepdims=True) + RMS_EPS) * g


def _bias_from_buckets(bucket, value_of):
    out = jnp.zeros(bucket.shape, F32)
    for b in range(N_BUCKETS):
        out = jnp.where(bucket == b, value_of(b), out)
    return out


def _inproj_sample_kernel(x_ref, g_ref, w_ref, qkv_ref, sk_ref, sv_ref, dk_ref, dv_ref, gate_ref):
    h = _rms(x_ref[...], g_ref[...]).astype(BF16)

    def sec(c):
        return _dot(h, w_ref[:, c * BRANCH_W:(c + 1) * BRANCH_W])

    for c, (scale, f32_out) in enumerate(((QK_SCALE, None), (1.0, sk_ref), (1.0, sv_ref),
                                          (QK_SCALE, None), (1.0, dk_ref), (1.0, dv_ref))):
        p = sec(c)
        if f32_out is not None:
            f32_out[...] = p
        qkv_ref[:, c * BRANCH_W:(c + 1) * BRANCH_W] = (p * scale).astype(BF16)
    for c in range(4):
        gate_ref[:, c * BRANCH_W:(c + 1) * BRANCH_W] = sec(6 + c)


def _inproj_sample(x, g, w_bf16, tm):
    n = x.shape[0]
    in_w = w_bf16.shape[1]
    kv = jax.ShapeDtypeStruct((n, BRANCH_W), F32)
    row = lambda i: (i, 0)
    return pl.pallas_call(
        _inproj_sample_kernel,
        grid=(n // tm,),
        in_specs=[pl.BlockSpec((tm, D_MODEL), row),
                  pl.BlockSpec((1, D_MODEL), lambda i: (0, 0)),
                  pl.BlockSpec((D_MODEL, in_w), lambda i: (0, 0))],
        out_specs=[pl.BlockSpec((tm, SAMPLE_QKV_W), row)] + [pl.BlockSpec((tm, BRANCH_W), row)] * 4
                  + [pl.BlockSpec((tm, 2 * D_MODEL), row)],
        out_shape=[jax.ShapeDtypeStruct((n, SAMPLE_QKV_W), BF16), kv, kv, kv, kv,
                   jax.ShapeDtypeStruct((n, 2 * D_MODEL), F32)],
        compiler_params=pltpu.CompilerParams(dimension_semantics=("arbitrary",),
                                             vmem_limit_bytes=VMEM_LIMIT),
        name="inproj_sample",
    )(x, g, w_bf16)


def _inproj_prompt_kernel(x_ref, g_ref, w_ref, wt_ref, qkv_ref, dv_ref, gate_ref,
                          skt_ref, svt_ref, dkt_ref, dqt_ref, dvt_ref):
    h = _rms(x_ref[...], g_ref[...]).astype(BF16)

    def sec(c):
        return _dot(h, w_ref[:, c * BRANCH_W:(c + 1) * BRANCH_W])

    def sec_t(c):
        return _nt_dot(wt_ref[c * BRANCH_W:(c + 1) * BRANCH_W, :], h)

    qkv_ref[:, 0 * BRANCH_W:1 * BRANCH_W] = (sec(0) * QK_SCALE).astype(BF16)
    for c in (1, 2, 3):
        qkv_ref[:, c * BRANCH_W:(c + 1) * BRANCH_W] = sec(c).astype(BF16)
    dv_ref[...] = sec(4)
    for c in range(4):
        gate_ref[:, c * BRANCH_W:(c + 1) * BRANCH_W] = sec(5 + c)
    skt_ref[0] = sec_t(0)
    svt_ref[0] = sec_t(1)
    dkt_ref[0] = sec_t(2)
    dqt_ref[...] = (sec_t(3) * (QK_SCALE * LOG2E)).astype(BF16)
    dvt_ref[0] = sec_t(4).astype(BF16)


def _inproj_prompt(x, g, w_bf16, wt_bf16, batch, seq, tm):
    n = x.shape[0]
    per_seq = seq // tm
    row = lambda i: (i, 0)
    const = lambda i: (0, 0)
    kvt = jax.ShapeDtypeStruct((batch, BRANCH_W, seq), F32)
    kvt_spec = pl.BlockSpec((1, BRANCH_W, tm), lambda i: (i // per_seq, 0, i % per_seq))
    return pl.pallas_call(
        _inproj_prompt_kernel,
        grid=(n // tm,),
        in_specs=[pl.BlockSpec((tm, D_MODEL), row), pl.BlockSpec((1, D_MODEL), const),
                  pl.BlockSpec(w_bf16.shape, const), pl.BlockSpec(wt_bf16.shape, const)],
        out_specs=[pl.BlockSpec((tm, PROMPT_QKV_W), row), pl.BlockSpec((tm, BRANCH_W), row),
                   pl.BlockSpec((tm, 2 * D_MODEL), row), kvt_spec, kvt_spec, kvt_spec,
                   pl.BlockSpec((BRANCH_W, tm), lambda i: (0, i)),
                   pl.BlockSpec((1, BRANCH_W, tm), lambda i: (i, 0, 0))],
        out_shape=[jax.ShapeDtypeStruct((n, PROMPT_QKV_W), BF16), jax.ShapeDtypeStruct((n, BRANCH_W), F32),
                   jax.ShapeDtypeStruct((n, 2 * D_MODEL), F32), kvt, kvt, kvt,
                   jax.ShapeDtypeStruct((BRANCH_W, n), BF16),
                   jax.ShapeDtypeStruct((n // tm, BRANCH_W, tm), BF16)],
        compiler_params=pltpu.CompilerParams(dimension_semantics=("arbitrary",),
                                             vmem_limit_bytes=VMEM_LIMIT),
        name="inproj_prompt",
    )(x, g, w_bf16, wt_bf16)


def _half_masks():
    lane = lax.broadcasted_iota(jnp.int32, (1, LANES), 1)
    return lane < HEAD_DIM, lane >= HEAD_DIM


def _stack_halves(q):
    lo, hi = _half_masks()
    zero = jnp.zeros_like(q)
    return jnp.concatenate([jnp.where(lo, q, zero), jnp.where(hi, q, zero)], axis=0)


def _strict_upper_ones(n):
    j = lax.broadcasted_iota(jnp.int32, (n, n), 0)
    s = lax.broadcasted_iota(jnp.int32, (n, n), 1)
    return jnp.where(j > s, 1.0, 0.0).astype(BF16)


def _sb_scores(z, mask, tri):
    sp = _softplus(z)
    log_keep = -sp if mask is None else jnp.where(mask, -sp, 0.0)
    return log_keep, z - sp + _split_dot(log_keep, tri)


def _sb_attn_kernel(q_ref, k_ref, v_ref, o_ref, acc_ref, car_ref, *, tq):
    i = pl.program_id(2)
    q2 = _stack_halves(q_ref[...])
    tri = _strict_upper_ones(tq)
    row = lax.broadcasted_iota(jnp.int32, (2 * tq, tq), 0)
    row = jnp.where(row >= tq, row - tq, row)
    col = lax.broadcasted_iota(jnp.int32, (2 * tq, tq), 1)
    acc_ref[...] = jnp.zeros_like(acc_ref)
    car_ref[...] = jnp.zeros_like(car_ref)

    def body(state):
        j, _ = state
        carry = car_ref[...]
        acc = acc_ref[...]
        for jb, live in ((j, None), (jnp.maximum(j - 1, 0), j >= 1)):
            ks = pl.multiple_of(jb * tq, tq)
            mask = (col + (jb - i) * tq) < row
            if live is not None:
                mask = mask & live
            log_keep, logit = _sb_scores(_nt_dot(q2, k_ref[pl.ds(ks, tq), :]), mask, tri)
            a = jnp.where(mask, jnp.exp(logit + carry), 0.0)
            acc = acc + _dot(a.astype(BF16), v_ref[pl.ds(ks, tq), :])
            carry = carry + jnp.sum(log_keep, axis=1, keepdims=True)
        acc_ref[...] = acc
        car_ref[...] = carry
        return j - 2, jnp.max(carry)

    lax.while_loop(lambda s: (s[0] >= 0) & (s[1] > -SB_STOP), body, (i, jnp.float32(0.0)))
    lo, _ = _half_masks()
    o_ref[...] = jnp.where(lo, acc_ref[:tq, :], acc_ref[tq:, :]).astype(o_ref.dtype)


def _sb_attn(qkv, batch, seq, tq):
    nq = seq // tq
    return pl.pallas_call(
        functools.partial(_sb_attn_kernel, tq=tq),
        grid=(batch, BLK, nq),
        in_specs=[pl.BlockSpec((tq, LANES), lambda b, p, i: (b * nq + i, COL_SQ + p)),
                  pl.BlockSpec((seq, LANES), lambda b, p, i: (b, COL_SK + p)),
                  pl.BlockSpec((seq, LANES), lambda b, p, i: (b, COL_SV + p))],
        out_specs=pl.BlockSpec((tq, LANES), lambda b, p, i: (b * nq + i, p)),
        out_shape=jax.ShapeDtypeStruct((batch * seq, BRANCH_W), BF16),
        scratch_shapes=[pltpu.VMEM((2 * tq, LANES), F32), pltpu.VMEM((2 * tq, 1), F32)],
        compiler_params=pltpu.CompilerParams(dimension_semantics=("arbitrary",) * 3,
                                             vmem_limit_bytes=VMEM_LIMIT),
        name="sb_attn",
    )(qkv, qkv, qkv)


def _lambda(lamv_ref, lambda_init):
    a = jnp.sum(lamv_ref[0:1, :] * lamv_ref[1:2, :], axis=1, keepdims=True)
    b = jnp.sum(lamv_ref[2:3, :] * lamv_ref[3:4, :], axis=1, keepdims=True)
    return jnp.exp(a) - jnp.exp(b) + lambda_init


def _df_attn_kernel(tab_ref, lamv_ref, qt_ref, k_ref, vt_ref, bkt_ref, g_ref, o_ref,
                    bias_s, m_s, l_s, acc_s, sa_s, sb_s, *, tq, lambda_init):
    h = pl.program_id(1)
    i = pl.program_id(2)

    @pl.when(i == 0)
    def _():
        far = tab_ref[N_BUCKETS - 1, h]
        for t in range(2):
            bias_s[t] = _bias_from_buckets(bkt_ref[t], lambda b: (tab_ref[b, h] - far) * LOG2E)

    qt = qt_ref[...].astype(F32)
    feat = lax.broadcasted_iota(jnp.int32, qt.shape, 0)
    q2t = jnp.concatenate([jnp.where(feat < HEAD_DIM, qt, 0.0), jnp.where(feat >= HEAD_DIM, qt, 0.0)],
                          axis=1).astype(BF16)
    m_s[...] = jnp.full_like(m_s, NEG_BIG)
    l_s[...] = jnp.zeros_like(l_s)
    acc_s[...] = jnp.zeros_like(acc_s)

    def score(j):
        return _dot(k_ref[pl.ds(pl.multiple_of(j * tq, tq), tq), :], q2t)

    def update(js, scores):
        m_old = m_s[...]
        m_new = m_old
        for s in scores:
            m_new = jnp.maximum(m_new, jnp.max(s, axis=0, keepdims=True))
        alpha = jnp.exp2(m_old - m_new)
        l_new = alpha * l_s[...]
        acc = alpha * acc_s[...]
        for j, s in zip(js, scores):
            p = jnp.exp2(s - m_new)
            l_new = l_new + jnp.sum(p, axis=0, keepdims=True)
            acc = acc + _dot(vt_ref[j], p.astype(BF16))
        l_s[...] = l_new
        acc_s[...] = acc
        m_s[...] = m_new

    def step(blocks):
        scores = []
        for j, bias, mask in blocks:
            s = score(j)
            if bias is not None:
                s = s + jnp.concatenate([bias, bias], axis=1)
            if mask is not None:
                s = jnp.where(mask, s, NEG_BIG)
            scores.append(s)
        update([j for j, _, _ in blocks], scores)

    n_far = jnp.maximum(i - 1, 0)
    quads = n_far // FAR_GROUP

    def scores_into(buf, quad):
        for k in range(FAR_GROUP):
            buf[k] = score(quad * FAR_GROUP + k)

    def update_from(buf, quad):
        update([quad * FAR_GROUP + k for k in range(FAR_GROUP)], [buf[k] for k in range(FAR_GROUP)])

    @pl.when(quads > 0)
    def _():
        scores_into(sa_s, 0)

    def far_two_groups(mm, c):
        scores_into(sb_s, 2 * mm + 1)
        update_from(sa_s, 2 * mm)
        scores_into(sa_s, jnp.minimum(2 * mm + 2, quads - 1))
        update_from(sb_s, 2 * mm + 1)
        return c

    lax.fori_loop(0, quads // 2, far_two_groups, 0)

    @pl.when(quads % 2 == 1)
    def _():
        update_from(sa_s, quads - 1)

    done = quads * FAR_GROUP

    @pl.when(n_far % FAR_GROUP >= 2)
    def _():
        step([(done, None, None), (done + 1, None, None)])

    @pl.when(n_far % 2 == 1)
    def _():
        step([(n_far - 1, None, None)])

    key = lax.broadcasted_iota(jnp.int32, (tq, 2 * tq), 0)
    qry = lax.broadcasted_iota(jnp.int32, (tq, 2 * tq), 1)
    qry = jnp.where(qry >= tq, qry - tq, qry)
    diag = (i, bias_s[0], key <= qry)

    @pl.when(i >= 1)
    def _():
        step([(i - 1, bias_s[1], None), diag])

    @pl.when(i == 0)
    def _():
        step([diag])

    lam = _lambda(lamv_ref, lambda_init)
    norm = acc_s[...] / l_s[...]
    out = (norm[:, :tq] - lam * norm[:, tq:]).T
    o_ref[...] = (_rms(out, g_ref[...]) * (1.0 - lambda_init)).astype(o_ref.dtype)


def _df_attn(tab, lamv, dqt, qkv, dvt, buckets, g_subln, batch, seq, tq, lambda_init):
    nq = seq // tq
    return pl.pallas_call(
        functools.partial(_df_attn_kernel, tq=tq, lambda_init=lambda_init),
        grid=(batch, DF_HEADS, nq),
        in_specs=[pl.BlockSpec(memory_space=pltpu.SMEM),
                  pl.BlockSpec((4, HEAD_DIM), lambda b, h, i: (0, 0)),
                  pl.BlockSpec((LANES, tq), lambda b, h, i: (h, b * nq + i)),
                  pl.BlockSpec((seq, LANES), lambda b, h, i: (b, COL_DK + h)),
                  pl.BlockSpec((nq, LANES, tq), lambda b, h, i: (b, h, 0)),
                  pl.BlockSpec((2, tq, tq), lambda b, h, i: (0, 0, 0)),
                  pl.BlockSpec((1, LANES), lambda b, h, i: (0, 0))],
        out_specs=pl.BlockSpec((tq, LANES), lambda b, h, i: (b * nq + i, h)),
        out_shape=jax.ShapeDtypeStruct((batch * seq, BRANCH_W), BF16),
        scratch_shapes=[pltpu.VMEM((2, tq, tq), F32), pltpu.VMEM((1, 2 * tq), F32),
                        pltpu.VMEM((1, 2 * tq), F32), pltpu.VMEM((LANES, 2 * tq), F32),
                        pltpu.VMEM((FAR_GROUP, tq, 2 * tq), F32), pltpu.VMEM((FAR_GROUP, tq, 2 * tq), F32)],
        compiler_params=pltpu.CompilerParams(dimension_semantics=("arbitrary",) * 3,
                                             vmem_limit_bytes=VMEM_LIMIT),
        name="df_attn",
    )(tab, lamv, dqt, qkv, dvt, buckets, g_subln)


def _decode_kernel(pt_ref, tab_ref, lamv_ref, qkv_ref, *rest, dq, page, lambda_init, group):
    ksb_hbm, vsb_hbm = rest[:2]
    kdf, vdf = (rest[2 + g * group:2 + (g + 1) * group] for g in range(2))
    (bkt_ref, g_ref, osb_ref, odf_ref, qsb_s, qdf_s, bias_s, acc_sb, car_s, acc_df, m_s, l_s,
     ksb_buf, vsb_buf, sb_sem, sb_live) = rest[2 + 2 * group:]
    b = pl.program_id(0)
    s = pl.program_id(1)
    n_steps = pl.num_programs(1)
    n_pages = (n_steps - 1) * group
    rows = SB_HEADS * dq
    hrows = rows // DF_HEADS
    tri = _strict_upper_ones(page)

    def sb_copies(step, slot):
        out = []
        for g in range(group):
            pg = pt_ref[b, n_pages - 1 - ((step - 1) * group + g)]
            src = pl.ds(pl.multiple_of(pg * BRANCH_W, BRANCH_W), BRANCH_W)
            out.append(pltpu.make_async_copy(ksb_hbm.at[src, :], ksb_buf.at[slot, g], sb_sem.at[slot]))
            out.append(pltpu.make_async_copy(vsb_hbm.at[src, :], vsb_buf.at[slot, g], sb_sem.at[slot]))
        return out

    def sb_prefetch(step):
        go = (step < n_steps) & (jnp.max(car_s[...]) > -SB_STOP)
        sb_live[0] = go.astype(jnp.int32)

        @pl.when(go)
        def _():
            for cp in sb_copies(step, step % 2):
                cp.start()

    def sb_update(z_list, mask, v_list, v_transposed):
        carry = car_s[...]
        acc = acc_sb[...]
        for z, v in zip(z_list, v_list):
            log_keep, logit = _sb_scores(z, mask, tri)
            a = jnp.exp(logit + carry)
            if mask is not None:
                a = jnp.where(mask, a, 0.0)
            a = a.astype(BF16)
            acc = acc + (_nt_dot(a, v) if v_transposed else _dot(a, v))
            carry = carry + jnp.sum(log_keep, axis=1, keepdims=True)
        acc_sb[...] = acc
        car_s[...] = carry

    def df_update(s_list, v_of):
        m_old = m_s[...]
        m_new = m_old
        for sc in s_list:
            m_new = jnp.maximum(m_new, jnp.max(sc, axis=1, keepdims=True))
        alpha = jnp.exp(m_old - m_new)
        p_list = [jnp.exp(sc - m_new) for sc in s_list]
        l_new = alpha * l_s[...]
        for p in p_list:
            l_new = l_new + jnp.sum(p, axis=1, keepdims=True)
        for h in range(DF_HEADS):
            sl = slice(h * hrows, (h + 1) * hrows)
            acc = alpha[sl] * acc_df[sl, :]
            for g, p in enumerate(p_list):
                acc = acc + _dot(p[sl].astype(BF16), v_of(g, h))
            acc_df[sl, :] = acc
        m_s[...] = m_new
        l_s[...] = l_new

    @pl.when(s == 0)
    def _():
        rq = lax.broadcasted_iota(jnp.int32, (rows, BRANCH_W), 0) // dq
        lq = lax.broadcasted_iota(jnp.int32, (rows, BRANCH_W), 1) // HEAD_DIM
        for c, dst in ((0, qsb_s), (3, qdf_s)):
            q = qkv_ref[0, :, c * BRANCH_W:(c + 1) * BRANCH_W].astype(F32)
            qt = jnp.concatenate([q] * SB_HEADS, axis=0)
            dst[...] = jnp.where(rq == lq, qt, 0.0).astype(BF16)
        for h in range(DF_HEADS):
            far = tab_ref[N_BUCKETS - 1, h]
            for t in range(2):
                bias_s[t, h * hrows:(h + 1) * hrows, :] = _bias_from_buckets(
                    bkt_ref[t, h * hrows:(h + 1) * hrows, :], lambda b: tab_ref[b, h] - far)
        acc_sb[...] = jnp.zeros_like(acc_sb)
        car_s[...] = jnp.zeros_like(car_s)
        acc_df[...] = jnp.zeros_like(acc_df)
        m_s[...] = jnp.full_like(m_s, NEG_BIG)
        l_s[...] = jnp.zeros_like(l_s)

        def new_rows(c):
            x = qkv_ref[0, :, c * BRANCH_W:(c + 1) * BRANCH_W].astype(F32)
            return jnp.concatenate([x, jnp.zeros((page - dq, BRANCH_W), F32)], axis=0).astype(BF16)

        row = lax.broadcasted_iota(jnp.int32, (rows, page), 0) % dq
        col = lax.broadcasted_iota(jnp.int32, (rows, page), 1)
        sb_update([_nt_dot(qsb_s[...], new_rows(1))], col < row, [new_rows(2)], False)
        v_new = new_rows(5)
        sc = jnp.where(col <= row, _nt_dot(qdf_s[...], new_rows(4)) + bias_s[0], NEG_BIG)
        df_update([sc], lambda g, h: v_new[:, h * LANES:(h + 1) * LANES])
        sb_prefetch(1)

    def cached_step(newest):
        @pl.when(sb_live[0] == 1)
        def _():
            slot = s % 2
            for cp in sb_copies(s, slot):
                cp.wait()
            q = qsb_s[...]
            sb_update([_dot(q, ksb_buf[slot, g].astype(BF16)) for g in range(group)], None,
                      [vsb_buf[slot, g].astype(BF16) for g in range(group)], True)
            sb_prefetch(s + 1)

        q = qdf_s[...]
        scores = [_dot(q, k[...].astype(BF16)) for k in kdf]
        if newest:
            scores[0] = scores[0] + bias_s[1]
        df_update(scores, lambda g, h: vdf[g][pl.ds(h, page, stride=DF_HEADS), :].astype(BF16))

    @pl.when(s == 1)
    def _():
        cached_step(True)

    @pl.when(s >= 2)
    def _():
        cached_step(False)

    @pl.when(s == n_steps - 1)
    def _():
        rq = lax.broadcasted_iota(jnp.int32, (rows, BRANCH_W), 0)
        lq = lax.broadcasted_iota(jnp.int32, (rows, BRANCH_W), 1)
        sb = jnp.where(rq // dq == lq // HEAD_DIM, acc_sb[...], 0.0)
        sb = functools.reduce(lambda a, b: a + b, [sb[g * dq:(g + 1) * dq, :] for g in range(SB_HEADS)])
        osb_ref[0] = sb.astype(osb_ref.dtype)

        lam = _lambda(lamv_ref, lambda_init)
        norm = acc_df[...] / l_s[...]
        g = g_ref[...]
        heads = []
        for h in range(DF_HEADS):
            o = norm[h * hrows:h * hrows + dq, :] - lam * norm[h * hrows + dq:(h + 1) * hrows, :]
            heads.append(_rms(o, g))
        odf_ref[0] = (jnp.concatenate(heads, axis=1) * (1.0 - lambda_init)).astype(odf_ref.dtype)


def _decode(page_table, tab, lamv, qkv3, caches, dec_buckets, g_subln, lambda_init, page):
    nb, dq, _ = qkv3.shape
    n_pages = page_table.shape[1]
    group = PAGES_PER_STEP
    while n_pages % group:
        group //= 2
    rows = SB_HEADS * dq

    def cache_spec(g):
        def index(b, s, pt):
            return (pt[b, n_pages - 1 - (jnp.maximum(s - 1, 0) * group + g)], 0)
        return pl.BlockSpec((BRANCH_W, page), index)

    cache_specs = [pl.BlockSpec(memory_space=pl.ANY)] * 2 + [cache_spec(g) for _ in range(2) for g in range(group)]
    cache_args = list(caches[:2]) + [c for c in caches[2:] for _ in range(group)]
    out = jax.ShapeDtypeStruct((nb, dq, BRANCH_W), BF16)
    const2 = lambda b, s, pt: (0, 0)
    grid_spec = pltpu.PrefetchScalarGridSpec(
        num_scalar_prefetch=1,
        grid=(nb, n_pages // group + 1),
        in_specs=[pl.BlockSpec(memory_space=pltpu.SMEM),
                  pl.BlockSpec((4, HEAD_DIM), const2),
                  pl.BlockSpec((1, dq, SAMPLE_QKV_W), lambda b, s, pt: (b, 0, 0))]
                 + cache_specs
                 + [pl.BlockSpec((2, rows, page), lambda b, s, pt: (0, 0, 0)),
                    pl.BlockSpec((1, LANES), const2)],
        out_specs=[pl.BlockSpec((1, dq, BRANCH_W), lambda b, s, pt: (b, 0, 0))] * 2,
        scratch_shapes=[pltpu.VMEM((rows, BRANCH_W), BF16), pltpu.VMEM((rows, BRANCH_W), BF16),
                        pltpu.VMEM((2, rows, page), F32),
                        pltpu.VMEM((rows, BRANCH_W), F32), pltpu.VMEM((rows, 1), F32),
                        pltpu.VMEM((rows, LANES), F32), pltpu.VMEM((rows, 1), F32),
                        pltpu.VMEM((rows, 1), F32),
                        pltpu.VMEM((2, group, BRANCH_W, page), F32), pltpu.VMEM((2, group, BRANCH_W, page), F32),
                        pltpu.SemaphoreType.DMA((2,)), pltpu.SMEM((1,), jnp.int32)],
    )
    return pl.pallas_call(
        functools.partial(_decode_kernel, dq=dq, page=page, lambda_init=lambda_init, group=group),
        grid_spec=grid_spec,
        out_shape=[out, out],
        compiler_params=pltpu.CompilerParams(dimension_semantics=("arbitrary", "arbitrary"),
                                             vmem_limit_bytes=VMEM_LIMIT),
        name="decode_attn",
    )(page_table, tab, lamv, qkv3, *cache_args, dec_buckets, g_subln)


def _post_kernel(sb_ref, df_ref, ga_ref, gb_ref, x_ref, bg_ref, wa_ref, wb_ref, wo_ref, gf_ref,
                 wrt_ref, brt_ref, cnt0_ref, x1_ref, h2_ref, comb_ref, pos_ref, cnt_ref, run_s):
    @pl.when(pl.program_id(0) == 0)
    def _():
        run_s[...] = cnt0_ref[...]

    a = _dot(sb_ref[...], wa_ref[...])
    b = _dot(df_ref[...], wb_ref[...])
    merged = jax.nn.sigmoid(ga_ref[...] + bg_ref[0:1, :]) * a + jax.nn.sigmoid(gb_ref[...] + bg_ref[1:2, :]) * b
    x1 = x_ref[...] + _dot(merged.astype(BF16), wo_ref[...])
    x1_ref[...] = x1
    h2 = _rms(x1, gf_ref[...])
    h2_ref[...] = h2.astype(BF16)

    wr = wrt_ref[...]
    wr_hi = wr.astype(BF16)
    wr_lo = (wr - wr_hi.astype(F32)).astype(BF16)
    h_hi = h2.astype(BF16)
    h_lo = (h2 - h_hi.astype(F32)).astype(BF16)
    logits = _nt_dot(wr_hi, h_hi) + _nt_dot(wr_lo, h_hi) + _nt_dot(wr_hi, h_lo) + brt_ref[...]

    expert = lax.broadcasted_iota(jnp.int32, logits.shape, 0).astype(F32)
    vals, hots = [], []
    for _ in range(TOP_K):
        mx = jnp.max(logits, axis=0, keepdims=True)
        first = jnp.min(jnp.where(logits == mx, expert, float(N_EXPERTS)), axis=0, keepdims=True)
        hot = expert == first
        vals.append(mx)
        hots.append(hot)
        logits = jnp.where(hot, -jnp.inf, logits)
    es = [jnp.exp(v - vals[0]) for v in vals]
    denom = functools.reduce(lambda u, v: u + v, es)
    comb = jnp.zeros(logits.shape, F32)
    routed = jnp.zeros(logits.shape, F32)
    for e, hot in zip(es, hots):
        comb = comb + jnp.where(hot, e / denom, 0.0)
        routed = routed + jnp.where(hot, 1.0, 0.0)
    comb_ref[...] = comb

    tm = logits.shape[1]
    j = lax.broadcasted_iota(jnp.int32, (tm, tm), 0)
    s = lax.broadcasted_iota(jnp.int32, (tm, tm), 1)
    earlier = jnp.where(j < s, 1.0, 0.0).astype(BF16)
    rank = _dot(routed.astype(BF16), earlier) + run_s[...]
    pos_ref[...] = jnp.where(routed > 0.0, rank, -1.0 - rank)
    run = run_s[...] + jnp.sum(routed, axis=1, keepdims=True)
    run_s[...] = run
    cnt_ref[...] = run


def _post(sb, df, gates, x, b_gate, wa, wb, wo, g_ffn, w_router_t, b_router_t, cnt0, tm):
    n = x.shape[0]
    row = lambda i: (i, 0)
    col = lambda i: (0, i)
    const = lambda i: (0, 0)
    routing = jax.ShapeDtypeStruct((N_EXPERTS, n), F32)
    return pl.pallas_call(
        _post_kernel,
        grid=(n // tm,),
        in_specs=[pl.BlockSpec((tm, BRANCH_W), row), pl.BlockSpec((tm, BRANCH_W), row),
                  pl.BlockSpec((tm, D_MODEL), lambda i: (i, 0)), pl.BlockSpec((tm, D_MODEL), lambda i: (i, 1)),
                  pl.BlockSpec((tm, D_MODEL), row),
                  pl.BlockSpec((2, D_MODEL), const),
                  pl.BlockSpec((BRANCH_W, D_MODEL), const), pl.BlockSpec((BRANCH_W, D_MODEL), const),
                  pl.BlockSpec((D_MODEL, D_MODEL), const),
                  pl.BlockSpec((1, D_MODEL), const),
                  pl.BlockSpec((N_EXPERTS, D_MODEL), const), pl.BlockSpec((N_EXPERTS, 1), const),
                  pl.BlockSpec((N_EXPERTS, 1), const)],
        out_specs=[pl.BlockSpec((tm, D_MODEL), row), pl.BlockSpec((tm, D_MODEL), row),
                   pl.BlockSpec((N_EXPERTS, tm), col), pl.BlockSpec((N_EXPERTS, tm), col),
                   pl.BlockSpec((N_EXPERTS, 1), const)],
        out_shape=[jax.ShapeDtypeStruct((n, D_MODEL), F32), jax.ShapeDtypeStruct((n, D_MODEL), BF16),
                   routing, routing, jax.ShapeDtypeStruct((N_EXPERTS, 1), F32)],
        scratch_shapes=[pltpu.VMEM((N_EXPERTS, 1), F32)],
        compiler_params=pltpu.CompilerParams(dimension_semantics=("arbitrary",),
                                             vmem_limit_bytes=VMEM_LIMIT),
        name="post_attn",
    )(sb, df, gates, gates, x, b_gate, wa, wb, wo, g_ffn, w_router_t, b_router_t, cnt0)


ROW_TILE = 256
TOKEN_TILE = 256
SLAB = 64
GATHER_ROWS = 128
GATHER_SPAN = 6
SLAB_ALIGN = 16


def _expert_rows_kernel(te_ref, q0_ref, wlo_ref, whi_ref, valid_ref,
                        h_ref, pos_ref, comb_ref, wi_ref, bi_ref, wo_ref, bo_ref, o_ref, x_s, g_s, *, span):
    r = pl.program_id(0)

    @pl.when(valid_ref[r] == 0)
    def _():
        o_ref[...] = jnp.zeros_like(o_ref)

    @pl.when(valid_ref[r] == 1)
    def _():
        e = te_ref[r]
        q0 = q0_ref[r].astype(F32)
        n_win = pos_ref.shape[1]
        parts = ROW_TILE // GATHER_ROWS
        lo = [wlo_ref[r * parts + p] for p in range(parts)]
        hi = [whi_ref[r * parts + p] for p in range(parts)]
        fits = functools.reduce(lambda a, b: a & b, [hi[p] - lo[p] <= span for p in range(parts)])

        def picks(rows, row0, w0, k):
            row = lax.broadcasted_iota(jnp.int32, (rows, TOKEN_TILE), 0).astype(F32) + (q0 + row0)
            hits = [pos_ref[e, pl.ds(w0 + i, 1), :] == row for i in range(k)]
            gsum = functools.reduce(lambda a, b: a + b,
                                    [jnp.where(h, comb_ref[e, pl.ds(w0 + i, 1), :], 0.0) for i, h in enumerate(hits)])
            return [jnp.where(h, 1.0, 0.0).astype(BF16) for h in hits], gsum

        @pl.when(fits)
        def _():
            for p in range(parts):
                rows = slice(p * GATHER_ROWS, (p + 1) * GATHER_ROWS)
                w0 = jnp.minimum(lo[p], n_win - span)
                onehot, gsum = picks(GATHER_ROWS, p * GATHER_ROWS, w0, span)
                tok0 = pl.multiple_of(w0 * TOKEN_TILE, TOKEN_TILE)
                x_s[rows, :] = _dot(jnp.concatenate(onehot, axis=1), h_ref[pl.ds(tok0, span * TOKEN_TILE), :])
                g_s[rows, :] = gsum

        @pl.when(jnp.logical_not(fits))
        def _():
            x_s[...] = jnp.zeros_like(x_s)
            g_s[...] = jnp.zeros_like(g_s)

            def window(w, c):
                onehot, gsum = picks(ROW_TILE, 0, w, 1)
                x_s[...] += _dot(onehot[0], h_ref[pl.ds(pl.multiple_of(w * TOKEN_TILE, TOKEN_TILE), TOKEN_TILE), :])
                g_s[...] += gsum
                return c

            lax.fori_loop(lo[0], hi[parts - 1], window, 0)

        h = x_s[...].astype(BF16)
        gate = jnp.minimum(_dot(h, wi_ref[0, :, :D_FF]) + bi_ref[0, :, :D_FF], SWIGLU_LIMIT)
        up = jnp.clip(_dot(h, wi_ref[0, :, D_FF:]) + bi_ref[0, :, D_FF:], -SWIGLU_LIMIT, SWIGLU_LIMIT)
        act = (up + 1.0) * gate * jax.nn.sigmoid(SWIGLU_ALPHA * gate)
        weight = jnp.sum(g_s[...], axis=1, keepdims=True)
        o_ref[...] = (weight * (_dot(act.astype(BF16), wo_ref[0]) + bo_ref[0])).astype(o_ref.dtype)


def _expert_rows(tile_meta, h2, pos3, comb3, wi, bi, wo, bo, n_row_tiles):
    n = h2.shape[0]
    once = pl.Buffered(1)
    exp3 = lambda r, te, *_: (te[r], 0, 0)
    span = min(GATHER_SPAN, pos3.shape[1])
    grid_spec = pltpu.PrefetchScalarGridSpec(
        num_scalar_prefetch=5,
        grid=(n_row_tiles,),
        in_specs=[pl.BlockSpec((n, D_MODEL), lambda r, *_: (0, 0), pipeline_mode=once),
                  pl.BlockSpec(pos3.shape, lambda r, *_: (0, 0, 0), pipeline_mode=once),
                  pl.BlockSpec(comb3.shape, lambda r, *_: (0, 0, 0), pipeline_mode=once),
                  pl.BlockSpec((1, D_MODEL, 2 * D_FF), exp3, pipeline_mode=once),
                  pl.BlockSpec((1, 1, 2 * D_FF), exp3),
                  pl.BlockSpec((1, D_FF, D_MODEL), exp3, pipeline_mode=once),
                  pl.BlockSpec((1, 1, D_MODEL), exp3)],
        out_specs=pl.BlockSpec((ROW_TILE, D_MODEL), lambda r, *_: (r, 0)),
        scratch_shapes=[pltpu.VMEM((ROW_TILE, D_MODEL), F32), pltpu.VMEM((ROW_TILE, TOKEN_TILE), F32)],
    )
    return pl.pallas_call(
        functools.partial(_expert_rows_kernel, span=span),
        grid_spec=grid_spec,
        out_shape=jax.ShapeDtypeStruct((n_row_tiles * ROW_TILE, D_MODEL), BF16),
        compiler_params=pltpu.CompilerParams(dimension_semantics=("arbitrary",),
                                             vmem_limit_bytes=VMEM_LIMIT),
        name="expert_rows",
    )(*tile_meta, h2, pos3, comb3, wi, bi, wo, bo)


def _combine_kernel(start_ref, nchunk_ref, pos_ref, off_ref, x1_ref, gfin_ref, rows_hbm, y_ref,
                    buf, sem, *, tile0, last_start):
    t = pl.program_id(0) + tile0
    pad = jnp.zeros((LANES - N_EXPERTS, TOKEN_TILE), F32)
    rank = jnp.concatenate([pos_ref[...], pad], axis=0).T
    dest = jnp.where(rank >= 0.0, rank + off_ref[...], -1.0)
    lane = lax.broadcasted_iota(jnp.int32, (1, 2 * SLAB), 1).astype(F32)

    def chunk(c, acc):
        starts = [start_ref[t * N_EXPERTS + e] + c * SLAB for e in range(N_EXPERTS)]
        copies = []
        for e in range(N_EXPERTS):
            src = pl.multiple_of(jnp.minimum(starts[e], last_start), SLAB_ALIGN)
            cp = pltpu.make_async_copy(rows_hbm.at[pl.ds(src, SLAB), :], buf.at[pl.ds(e * SLAB, SLAB), :], sem)
            cp.start()
            copies.append(cp)
        for cp in copies:
            cp.wait()
        pieces = []
        for e in range(0, N_EXPERTS, 2):
            rel = []
            for k in range(2):
                d = dest[:, e + k:e + k + 1] - starts[e + k].astype(F32)
                rel.append(jnp.where((d >= 0.0) & (d < SLAB), d + k * SLAB, -1.0))
            hit = (rel[0] == lane) | (rel[1] == lane)
            pieces.append(jnp.where(hit, 1.0, 0.0).astype(BF16))
        return acc + _dot(jnp.concatenate(pieces, axis=1), buf[...])

    acc = lax.fori_loop(0, nchunk_ref[t], chunk, jnp.zeros((TOKEN_TILE, D_MODEL), F32))
    y_ref[...] = _rms(x1_ref[...] + acc, gfin_ref[...])


def _combine(slab_start, n_chunks, pos_t, off_row, x1, g_final, rows, tile0):
    n = x1.shape[0]
    n_rows = rows.shape[0]
    grid_spec = pltpu.PrefetchScalarGridSpec(
        num_scalar_prefetch=2,
        grid=(n // TOKEN_TILE,),
        in_specs=[pl.BlockSpec((N_EXPERTS, TOKEN_TILE), lambda i, *_: (0, i + tile0)),
                  pl.BlockSpec((1, LANES), lambda i, *_: (0, 0)),
                  pl.BlockSpec((TOKEN_TILE, D_MODEL), lambda i, *_: (i, 0)),
                  pl.BlockSpec((1, D_MODEL), lambda i, *_: (0, 0)),
                  pl.BlockSpec(memory_space=pl.ANY)],
        out_specs=pl.BlockSpec((TOKEN_TILE, D_MODEL), lambda i, *_: (i, 0)),
        scratch_shapes=[pltpu.VMEM((N_EXPERTS * SLAB, D_MODEL), BF16), pltpu.SemaphoreType.DMA(())],
    )
    return pl.pallas_call(
        functools.partial(_combine_kernel, tile0=tile0, last_start=n_rows - SLAB),
        grid_spec=grid_spec,
        out_shape=jax.ShapeDtypeStruct((n, D_MODEL), F32),
        compiler_params=pltpu.CompilerParams(dimension_semantics=("arbitrary",),
                                             vmem_limit_bytes=VMEM_LIMIT),
        name="combine",
    )(slab_start, n_chunks, pos_t, off_row, x1, g_final, rows)


def _routing_tables(pos_t, counts, n_tokens):
    n_win = n_tokens // TOKEN_TILE
    max_tiles = (TOP_K * n_tokens + N_EXPERTS * (ROW_TILE - 1)) // ROW_TILE
    n_row_tiles = max_tiles + 1
    counts = counts.astype(jnp.int32)
    tiles_e = (counts + ROW_TILE - 1) // ROW_TILE
    tile_end = jnp.cumsum(tiles_e)
    tile_start = tile_end - tiles_e
    off = tile_start * ROW_TILE
    r = jnp.arange(n_row_tiles, dtype=jnp.int32)
    te = jnp.minimum(jnp.sum(r[:, None] >= tile_end[None, :], axis=1), N_EXPERTS - 1).astype(jnp.int32)
    valid = (r < tile_end[-1]).astype(jnp.int32)
    mine = (te[:, None] == jnp.arange(N_EXPERTS, dtype=jnp.int32)[None, :]).astype(jnp.int32)
    q0 = (r - jnp.sum(mine * tile_start[None, :], axis=1)) * ROW_TILE
    first = pos_t[:, ::TOKEN_TILE]
    before = jnp.where(first < 0, -1.0 - first, first).astype(jnp.int32)
    edges = jnp.concatenate([before, counts[:, None]], axis=1)
    edges_r = jnp.sum(mine[:, :, None] * edges[None, :, :], axis=1)
    e_lo, e_hi = edges_r[:, :-1], edges_r[:, 1:]
    part_q0 = q0[:, None] + jnp.arange(0, ROW_TILE, GATHER_ROWS, dtype=jnp.int32)[None, :]
    wlo = jnp.sum(e_hi[:, None, :] <= part_q0[:, :, None], axis=2).astype(jnp.int32) * valid[:, None]
    whi = jnp.sum(e_lo[:, None, :] < (part_q0 + GATHER_ROWS)[:, :, None], axis=2).astype(jnp.int32) * valid[:, None]
    wlo, whi = wlo.reshape(-1), whi.reshape(-1)
    start = off[:, None] + before
    start_al = (start // SLAB_ALIGN) * SLAB_ALIGN
    need = start - start_al + (edges[:, 1:] - edges[:, :-1])
    n_chunks = jnp.maximum(jnp.max((need + SLAB - 1) // SLAB, axis=0), 1).astype(jnp.int32)
    slab_start = start_al.T.reshape(-1).astype(jnp.int32)
    off_row = jnp.pad(off.astype(F32), (0, LANES - N_EXPERTS))[None, :]
    return (te, q0.astype(jnp.int32), wlo, whi, valid), n_row_tiles, slab_start, n_chunks, off_row


def _bucket_np(n):
    max_exact = N_BUCKETS // 2
    nf = np.maximum(n, max_exact).astype(np.float32)
    large = max_exact + (np.log(nf / np.float32(max_exact)) / np.float32(math.log(MAX_DISTANCE / max_exact))
                         * np.float32(N_BUCKETS - max_exact)).astype(np.int32)
    return np.where(n < max_exact, n, np.minimum(large, N_BUCKETS - 1)).astype(np.int32)


def _tile(n, cap):
    t = cap
    while n % t:
        t //= 2
    return t


def kernel(x_prompt, x_sample, cache_sb_k, cache_sb_v, cache_df_k, cache_df_v, page_table, rel_bias, g_attn, w_in, b_gate, df_lambda_q1, df_lambda_k1, df_lambda_q2, df_lambda_k2, g_subln, w_proj_a, w_proj_b, w_out, g_ffn, w_router, b_router, w_exp_in, b_exp_in, w_exp_out, b_exp_out, g_final):
    depth = w_in.shape[0]
    assert depth == 1, "single-layer step"
    batch, seq, _ = x_prompt.shape
    nb, dq, _ = x_sample.shape
    n_pool, page = cache_sb_k.shape[1], cache_sb_k.shape[2]
    tq = _tile(seq, 256)
    assert tq >= MAX_DISTANCE and page >= MAX_DISTANCE, "bias is constant beyond the two nearest tiles"
    lambda_init = 0.8 - 0.6 * math.exp(-0.3 * 0)

    w = w_in[0]
    sec = lambda c: w[:, c * BRANCH_W:(c + 1) * BRANCH_W]
    w_in_b = w.astype(BF16)
    w_rows = jnp.concatenate([sec(0), sec(1), sec(2), sec(4), sec(5), w[:, 6 * BRANCH_W:]], axis=1).astype(BF16)
    w_cols = jnp.concatenate([sec(1), sec(2), sec(4), sec(3), sec(5)], axis=1).T.astype(BF16)
    wa, wb, wo = w_proj_a[0].astype(BF16), w_proj_b[0].astype(BF16), w_out[0].astype(BF16)
    wi, wo_e = w_exp_in[0].astype(BF16), w_exp_out[0].astype(BF16)
    bi, bo = b_exp_in[0][:, None, :], b_exp_out[0][:, None, :]
    lamv = jnp.stack([df_lambda_q1[0], df_lambda_k1[0], df_lambda_q2[0], df_lambda_k2[0]]).astype(F32)
    g_sub = g_subln[0][None, :]
    g_a, g_f, g_fin = g_attn[0][None, :], g_ffn[0][None, :], g_final[None, :]
    b_r = b_router[0][None, :]

    k_i = np.arange(tq)[:, None]
    q_i = np.arange(tq)[None, :]
    near_buckets = jnp.asarray(_bucket_np(np.stack([np.maximum(q_i - k_i, 0), q_i - k_i + tq])))
    r_i = (np.arange(SB_HEADS * dq) % dq)[:, None]
    c_i = np.arange(page)[None, :]
    dec_buckets = jnp.asarray(_bucket_np(np.stack([np.maximum(r_i - c_i, 0), r_i + page - c_i])))

    xp = x_prompt.reshape(batch * seq, D_MODEL)
    xs = x_sample.reshape(nb * dq, D_MODEL)

    qkv_p, dv_p, gates_p, skt, svt, dkt, dqt, dvt = _inproj_prompt(xp, g_a, w_rows, w_cols, batch, seq, tq)
    qkv_s, sk_s, sv_s, dk_s, dv_s, gates_s = _inproj_sample(xs, g_a, w_in_b, _tile(nb * dq, 256))

    sb_p = _sb_attn(qkv_p, batch, seq, tq)
    df_p = _df_attn(rel_bias, lamv, dqt, qkv_p, dvt, near_buckets, g_sub, batch, seq, tq, lambda_init)

    rows_of_pool = n_pool * BRANCH_W
    caches = [jnp.transpose(cache_sb_k[0], (0, 2, 3, 1)).reshape(rows_of_pool, page),
              jnp.transpose(cache_sb_v[0], (0, 2, 3, 1)).reshape(rows_of_pool, page),
              jnp.transpose(cache_df_k[0], (0, 2, 3, 4, 1)).reshape(rows_of_pool, page),
              cache_df_v[0].reshape(rows_of_pool, LANES)]
    sb_s, df_s = _decode(page_table, rel_bias, lamv, qkv_s.reshape(nb, dq, SAMPLE_QKV_W), caches, dec_buckets,
                         g_sub, lambda_init, page)
    sb_s = sb_s.reshape(nb * dq, BRANCH_W)
    df_s = df_s.reshape(nb * dq, BRANCH_W)

    n_p, n_s = batch * seq, nb * dq
    assert n_p % TOKEN_TILE == 0 and n_s % TOKEN_TILE == 0
    w_rt, b_rt = w_router[0].T, b_router[0][:, None]
    post = functools.partial(_post, b_gate=b_gate[0], wa=wa, wb=wb, wo=wo, g_ffn=g_f, w_router_t=w_rt, b_router_t=b_rt)
    x1_p, h2_p, comb_p, pos_p, cnt_p = post(sb_p, df_p, gates_p, xp, cnt0=jnp.zeros((N_EXPERTS, 1), F32),
                                            tm=_tile(n_p, 512))
    x1_s, h2_s, comb_s, pos_s, cnt_all = post(sb_s, df_s, gates_s, xs, cnt0=cnt_p, tm=_tile(n_s, 512))
    h2 = jnp.concatenate([h2_p, h2_s], axis=0)
    comb_t = jnp.concatenate([comb_p, comb_s], axis=1)
    pos_t = jnp.concatenate([pos_p, pos_s], axis=1)
    n_all = n_p + n_s
    tile_meta, n_row_tiles, slab_start, n_chunks, off_row = _routing_tables(pos_t, cnt_all[:, 0], n_all)
    by_window = lambda a: a.reshape(N_EXPERTS, n_all // TOKEN_TILE, TOKEN_TILE)
    rows = _expert_rows(tile_meta, h2, by_window(pos_t), by_window(comb_t), wi, bi, wo_e, bo, n_row_tiles)
    y_p = _combine(slab_start, n_chunks, pos_t, off_row, x1_p, g_fin, rows, 0)
    y_s = _combine(slab_start, n_chunks, pos_t, off_row, x1_s, g_fin, rows, n_p // TOKEN_TILE)
    y_p = y_p.reshape(batch, seq, D_MODEL)
    y_s = y_s.reshape(nb, dq, D_MODEL)

    def from_t(a, feat_shape):
        a = a.reshape((batch,) + feat_shape + (seq,))
        return jnp.moveaxis(a, -1, 1)[None]

    s_lead = (depth, nb, dq)
    return (y_p, y_s,
            from_t(skt, (SB_HEADS, HEAD_DIM)), from_t(svt, (SB_HEADS, HEAD_DIM)),
            from_t(dkt, (DF_HEADS, 2, HEAD_DIM)), dv_p.reshape(depth, batch, seq, DF_HEADS, 2 * HEAD_DIM),
            sk_s.reshape(s_lead + (SB_HEADS, HEAD_DIM)), sv_s.reshape(s_lead + (SB_HEADS, HEAD_DIM)),
            dk_s.reshape(s_lead + (DF_HEADS, 2, HEAD_DIM)), dv_s.reshape(s_lead + (DF_HEADS, 2 * HEAD_DIM)))
```

```python
import functools
import math

import numpy as np
import jax
import jax.numpy as jnp
from jax import lax
from jax.experimental import pallas as pl
from jax.experimental.pallas import tpu as pltpu

F32 = jnp.float32
BF16 = jnp.bfloat16

D_MODEL = 1024
SB_HEADS = 8
DF_HEADS = 4
HEAD_DIM = 64
BRANCH_W = 512
LANES = 128
N_BUCKETS = 32
MAX_DISTANCE = 128
N_EXPERTS = 32
TOP_K = 4
D_FF = 1024
SWIGLU_LIMIT = 7.0
SWIGLU_ALPHA = 1.702
RMS_EPS = 1e-5
QK_SCALE = HEAD_DIM ** -0.5
LOG2E = math.log2(math.e)

BLK = BRANCH_W // LANES
COL_SQ, COL_SK, COL_SV, COL_DK = 0, BLK, 2 * BLK, 3 * BLK
SAMPLE_QKV_W = 6 * BRANCH_W
PROMPT_QKV_W = 4 * BRANCH_W

SB_STOP = 110.0
NEG_BIG = -1e30
FAR_GROUP = 4
VMEM_LIMIT = 56 * 1024 * 1024
PAGES_PER_STEP = 8


def _nt_dot(a, b):
    return lax.dot_general(a, b, (((1,), (1,)), ((), ())), preferred_element_type=F32)


def _dot(a, b):
    return jnp.dot(a, b, preferred_element_type=F32)


def _split_dot(x, w_bf16):
    hi = x.astype(BF16)
    lo = (x - hi.astype(F32)).astype(BF16)
    return _dot(hi, w_bf16) + _dot(lo, w_bf16)


def _softplus(z):
    return jnp.maximum(z, 0.0) + jnp.log1p(jnp.exp(-jnp.abs(z)))


def _rms(x, g):
    return x * lax.rsqrt(jnp.mean(x * x, axis=-1, keepdims=True) + RMS_EPS) * g


def _bias_from_buckets(bucket, value_of):
    out = jnp.zeros(bucket.shape, F32)
    for b in range(N_BUCKETS):
        out = jnp.where(bucket == b, value_of(b), out)
    return out


def _inproj_sample_kernel(x_ref, g_ref, w_ref, qkv_ref, sk_ref, sv_ref, dk_ref, dv_ref, gate_ref):
    h = _rms(x_ref[...], g_ref[...]).astype(BF16)

    def sec(c):
        return _dot(h, w_ref[:, c * BRANCH_W:(c + 1) * BRANCH_W])

    for c, (scale, f32_out) in enumerate(((QK_SCALE, None), (1.0, sk_ref), (1.0, sv_ref),
                                          (QK_SCALE, None), (1.0, dk_ref), (1.0, dv_ref))):
        p = sec(c)
        if f32_out is not None:
            f32_out[...] = p
        qkv_ref[:, c * BRANCH_W:(c + 1) * BRANCH_W] = (p * scale).astype(BF16)
    for c in range(4):
        gate_ref[:, c * BRANCH_W:(c + 1) * BRANCH_W] = sec(6 + c)


def _inproj_sample(x, g, w_bf16, tm):
    n = x.shape[0]
    in_w = w_bf16.shape[1]
    kv = jax.ShapeDtypeStruct((n, BRANCH_W), F32)
    row = lambda i: (i, 0)
    return pl.pallas_call(
        _inproj_sample_kernel,
        grid=(n // tm,),
        in_specs=[pl.BlockSpec((tm, D_MODEL), row),
                  pl.BlockSpec((1, D_MODEL), lambda i: (0, 0)),
                  pl.BlockSpec((D_MODEL, in_w), lambda i: (0, 0))],
        out_specs=[pl.BlockSpec((tm, SAMPLE_QKV_W), row)] + [pl.BlockSpec((tm, BRANCH_W), row)] * 4
                  + [pl.BlockSpec((tm, 2 * D_MODEL), row)],
        out_shape=[jax.ShapeDtypeStruct((n, SAMPLE_QKV_W), BF16), kv, kv, kv, kv,
                   jax.ShapeDtypeStruct((n, 2 * D_MODEL), F32)],
        compiler_params=pltpu.CompilerParams(dimension_semantics=("arbitrary",),
                                             vmem_limit_bytes=VMEM_LIMIT),
        name="inproj_sample",
    )(x, g, w_bf16)


def _inproj_prompt_kernel(x_ref, g_ref, w_ref, wt_ref, qkv_ref, dv_ref, gate_ref,
                          skt_ref, svt_ref, dkt_ref, dqt_ref, dvt_ref):
    h = _rms(x_ref[...], g_ref[...]).astype(BF16)

    def sec(c):
        return _dot(h, w_ref[:, c * BRANCH_W:(c + 1) * BRANCH_W])

    def sec_t(c):
        return _nt_dot(wt_ref[c * BRANCH_W:(c + 1) * BRANCH_W, :], h)

    qkv_ref[:, 0 * BRANCH_W:1 * BRANCH_W] = (sec(0) * QK_SCALE).astype(BF16)
    for c in (1, 2, 3):
        qkv_ref[:, c * BRANCH_W:(c + 1) * BRANCH_W] = sec(c).astype(BF16)
    dv_ref[...] = sec(4)
    for c in range(4):
        gate_ref[:, c * BRANCH_W:(c + 1) * BRANCH_W] = sec(5 + c)
    skt_ref[0] = sec_t(0)
    svt_ref[0] = sec_t(1)
    dkt_ref[0] = sec_t(2)
    dqt_ref[...] = (sec_t(3) * (QK_SCALE * LOG2E)).astype(BF16)
    dvt_ref[0] = sec_t(4).astype(BF16)


def _inproj_prompt(x, g, w_bf16, wt_bf16, batch, seq, tm):
    n = x.shape[0]
    per_seq = seq // tm
    row = lambda i: (i, 0)
    const = lambda i: (0, 0)
    kvt = jax.ShapeDtypeStruct((batch, BRANCH_W, seq), F32)
    kvt_spec = pl.BlockSpec((1, BRANCH_W, tm), lambda i: (i // per_seq, 0, i % per_seq))
    return pl.pallas_call(
        _inproj_prompt_kernel,
        grid=(n // tm,),
        in_specs=[pl.BlockSpec((tm, D_MODEL), row), pl.BlockSpec((1, D_MODEL), const),
                  pl.BlockSpec(w_bf16.shape, const), pl.BlockSpec(wt_bf16.shape, const)],
        out_specs=[pl.BlockSpec((tm, PROMPT_QKV_W), row), pl.BlockSpec((tm, BRANCH_W), row),
                   pl.BlockSpec((tm, 2 * D_MODEL), row), kvt_spec, kvt_spec, kvt_spec,
                   pl.BlockSpec((BRANCH_W, tm), lambda i: (0, i)),
                   pl.BlockSpec((1, BRANCH_W, tm), lambda i: (i, 0, 0))],
        out_shape=[jax.ShapeDtypeStruct((n, PROMPT_QKV_W), BF16), jax.ShapeDtypeStruct((n, BRANCH_W), F32),
                   jax.ShapeDtypeStruct((n, 2 * D_MODEL), F32), kvt, kvt, kvt,
                   jax.ShapeDtypeStruct((BRANCH_W, n), BF16),
                   jax.ShapeDtypeStruct((n // tm, BRANCH_W, tm), BF16)],
        compiler_params=pltpu.CompilerParams(dimension_semantics=("arbitrary",),
                                             vmem_limit_bytes=VMEM_LIMIT),
        name="inproj_prompt",
    )(x, g, w_bf16, wt_bf16)


def _half_masks():
    lane = lax.broadcasted_iota(jnp.int32, (1, LANES), 1)
    return lane < HEAD_DIM, lane >= HEAD_DIM


def _stack_halves(q):
    lo, hi = _half_masks()
    zero = jnp.zeros_like(q)
    return jnp.concatenate([jnp.where(lo, q, zero), jnp.where(hi, q, zero)], axis=0)


def _strict_upper_ones(n):
    j = lax.broadcasted_iota(jnp.int32, (n, n), 0)
    s = lax.broadcasted_iota(jnp.int32, (n, n), 1)
    return jnp.where(j > s, 1.0, 0.0).astype(BF16)


def _sb_scores(z, mask, tri):
    sp = _softplus(z)
    log_keep = -sp if mask is None else jnp.where(mask, -sp, 0.0)
    return log_keep, z - sp + _split_dot(log_keep, tri)


def _sb_attn_kernel(q_ref, k_ref, v_ref, o_ref, acc_ref, car_ref, *, tq):
    i = pl.program_id(2)
    q2 = _stack_halves(q_ref[...])
    tri = _strict_upper_ones(tq)
    row = lax.broadcasted_iota(jnp.int32, (2 * tq, tq), 0)
    row = jnp.where(row >= tq, row - tq, row)
    col = lax.broadcasted_iota(jnp.int32, (2 * tq, tq), 1)
    acc_ref[...] = jnp.zeros_like(acc_ref)
    car_ref[...] = jnp.zeros_like(car_ref)

    def body(state):
        j, _ = state
        carry = car_ref[...]
        acc = acc_ref[...]
        for jb, live in ((j, None), (jnp.maximum(j - 1, 0), j >= 1)):
            ks = pl.multiple_of(jb * tq, tq)
            mask = (col + (jb - i) * tq) < row
            if live is not None:
                mask = mask & live
            log_keep, logit = _sb_scores(_nt_dot(q2, k_ref[pl.ds(ks, tq), :]), mask, tri)
            a = jnp.where(mask, jnp.exp(logit + carry), 0.0)
            acc = acc + _dot(a.astype(BF16), v_ref[pl.ds(ks, tq), :])
            carry = carry + jnp.sum(log_keep, axis=1, keepdims=True)
        acc_ref[...] = acc
        car_ref[...] = carry
        return j - 2, jnp.max(carry)

    lax.while_loop(lambda s: (s[0] >= 0) & (s[1] > -SB_STOP), body, (i, jnp.float32(0.0)))
    lo, _ = _half_masks()
    o_ref[...] = jnp.where(lo, acc_ref[:tq, :], acc_ref[tq:, :]).astype(o_ref.dtype)


def _sb_attn(qkv, batch, seq, tq):
    nq = seq // tq
    return pl.pallas_call(
        functools.partial(_sb_attn_kernel, tq=tq),
        grid=(batch, BLK, nq),
        in_specs=[pl.BlockSpec((tq, LANES), lambda b, p, i: (b * nq + i, COL_SQ + p)),
                  pl.BlockSpec((seq, LANES), lambda b, p, i: (b, COL_SK + p)),
                  pl.BlockSpec((seq, LANES), lambda b, p, i: (b, COL_SV + p))],
        out_specs=pl.BlockSpec((tq, LANES), lambda b, p, i: (b * nq + i, p)),
        out_shape=jax.ShapeDtypeStruct((batch * seq, BRANCH_W), BF16),
        scratch_shapes=[pltpu.VMEM((2 * tq, LANES), F32), pltpu.VMEM((2 * tq, 1), F32)],
        compiler_params=pltpu.CompilerParams(dimension_semantics=("arbitrary",) * 3,
                                             vmem_limit_bytes=VMEM_LIMIT),
        name="sb_attn",
    )(qkv, qkv, qkv)


def _lambda(lamv_ref, lambda_init):
    a = jnp.sum(lamv_ref[0:1, :] * lamv_ref[1:2, :], axis=1, keepdims=True)
    b = jnp.sum(lamv_ref[2:3, :] * lamv_ref[3:4, :], axis=1, keepdims=True)
    return jnp.exp(a) - jnp.exp(b) + lambda_init


def _df_attn_kernel(tab_ref, lamv_ref, qt_ref, k_ref, vt_ref, bkt_ref, g_ref, o_ref,
                    bias_s, m_s, l_s, acc_s, sa_s, sb_s, *, tq, lambda_init):
    h = pl.program_id(1)
    i = pl.program_id(2)

    @pl.when(i == 0)
    def _():
        far = tab_ref[N_BUCKETS - 1, h]
        for t in range(2):
            bias_s[t] = _bias_from_buckets(bkt_ref[t], lambda b: (tab_ref[b, h] - far) * LOG2E)

    qt = qt_ref[...].astype(F32)
    feat = lax.broadcasted_iota(jnp.int32, qt.shape, 0)
    q2t = jnp.concatenate([jnp.where(feat < HEAD_DIM, qt, 0.0), jnp.where(feat >= HEAD_DIM, qt, 0.0)],
                          axis=1).astype(BF16)
    m_s[...] = jnp.full_like(m_s, NEG_BIG)
    l_s[...] = jnp.zeros_like(l_s)
    acc_s[...] = jnp.zeros_like(acc_s)

    def score(j):
        return _dot(k_ref[pl.ds(pl.multiple_of(j * tq, tq), tq), :], q2t)

    def update(js, scores):
        m_old = m_s[...]
        m_new = m_old
        for s in scores:
            m_new = jnp.maximum(m_new, jnp.max(s, axis=0, keepdims=True))
        alpha = jnp.exp2(m_old - m_new)
        l_new = alpha * l_s[...]
        acc = alpha * acc_s[...]
        for j, s in zip(js, scores):
            p = jnp.exp2(s - m_new)
            l_new = l_new + jnp.sum(p, axis=0, keepdims=True)
            acc = acc + _dot(vt_ref[j], p.astype(BF16))
        l_s[...] = l_new
        acc_s[...] = acc
        m_s[...] = m_new

    def step(blocks):
        scores = []
        for j, bias, mask in blocks:
            s = score(j)
            if bias is not None:
                s = s + jnp.concatenate([bias, bias], axis=1)
            if mask is not None:
                s = jnp.where(mask, s, NEG_BIG)
            scores.append(s)
        update([j for j, _, _ in blocks], scores)

    n_far = jnp.maximum(i - 1, 0)
    quads = n_far // FAR_GROUP

    def scores_into(buf, quad):
        for k in range(FAR_GROUP):
            buf[k] = score(quad * FAR_GROUP + k)

    def update_from(buf, quad):
        update([quad * FAR_GROUP + k for k in range(FAR_GROUP)], [buf[k] for k in range(FAR_GROUP)])

    @pl.when(quads > 0)
    def _():
        scores_into(sa_s, 0)

    def far_two_groups(mm, c):
        scores_into(sb_s, 2 * mm + 1)
        update_from(sa_s, 2 * mm)
        scores_into(sa_s, jnp.minimum(2 * mm + 2, quads - 1))
        update_from(sb_s, 2 * mm + 1)
        return c

    lax.fori_loop(0, quads // 2, far_two_groups, 0)

    @pl.when(quads % 2 == 1)
    def _():
        update_from(sa_s, quads - 1)

    done = quads * FAR_GROUP

    @pl.when(n_far % FAR_GROUP >= 2)
    def _():
        step([(done, None, None), (done + 1, None, None)])

    @pl.when(n_far % 2 == 1)
    def _():
        step([(n_far - 1, None, None)])

    key = lax.broadcasted_iota(jnp.int32, (tq, 2 * tq), 0)
    qry = lax.broadcasted_iota(jnp.int32, (tq, 2 * tq), 1)
    qry = jnp.where(qry >= tq, qry - tq, qry)
    diag = (i, bias_s[0], key <= qry)

    @pl.when(i >= 1)
    def _():
        step([(i - 1, bias_s[1], None), diag])

    @pl.when(i == 0)
    def _():
        step([diag])

    lam = _lambda(lamv_ref, lambda_init)
    norm = acc_s[...] / l_s[...]
    out = (norm[:, :tq] - lam * norm[:, tq:]).T
    o_ref[...] = (_rms(out, g_ref[...]) * (1.0 - lambda_init)).astype(o_ref.dtype)


def _df_attn(tab, lamv, dqt, qkv, dvt, buckets, g_subln, batch, seq, tq, lambda_init):
    nq = seq // tq
    return pl.pallas_call(
        functools.partial(_df_attn_kernel, tq=tq, lambda_init=lambda_init),
        grid=(batch, DF_HEADS, nq),
        in_specs=[pl.BlockSpec(memory_space=pltpu.SMEM),
                  pl.BlockSpec((4, HEAD_DIM), lambda b, h, i: (0, 0)),
                  pl.BlockSpec((LANES, tq), lambda b, h, i: (h, b * nq + i)),
                  pl.BlockSpec((seq, LANES), lambda b, h, i: (b, COL_DK + h)),
                  pl.BlockSpec((nq, LANES, tq), lambda b, h, i: (b, h, 0)),
                  pl.BlockSpec((2, tq, tq), lambda b, h, i: (0, 0, 0)),
                  pl.BlockSpec((1, LANES), lambda b, h, i: (0, 0))],
        out_specs=pl.BlockSpec((tq, LANES), lambda b, h, i: (b * nq + i, h)),
        out_shape=jax.ShapeDtypeStruct((batch * seq, BRANCH_W), BF16),
        scratch_shapes=[pltpu.VMEM((2, tq, tq), F32), pltpu.VMEM((1, 2 * tq), F32),
                        pltpu.VMEM((1, 2 * tq), F32), pltpu.VMEM((LANES, 2 * tq), F32),
                        pltpu.VMEM((FAR_GROUP, tq, 2 * tq), F32), pltpu.VMEM((FAR_GROUP, tq, 2 * tq), F32)],
        compiler_params=pltpu.CompilerParams(dimension_semantics=("arbitrary",) * 3,
                                             vmem_limit_bytes=VMEM_LIMIT),
        name="df_attn",
    )(tab, lamv, dqt, qkv, dvt, buckets, g_subln)


def _decode_kernel(pt_ref, tab_ref, lamv_ref, qkv_ref, *rest, dq, page, lambda_init, group):
    ksb_hbm, vsb_hbm = rest[:2]
    kdf, vdf = (rest[2 + g * group:2 + (g + 1) * group] for g in range(2))
    (bkt_ref, g_ref, osb_ref, odf_ref, qsb_s, qdf_s, bias_s, acc_sb, car_s, acc_df, m_s, l_s,
     ksb_buf, vsb_buf, sb_sem, sb_live) = rest[2 + 2 * group:]
    b = pl.program_id(0)
    s = pl.program_id(1)
    n_steps = pl.num_programs(1)
    n_pages = (n_steps - 1) * group
    rows = SB_HEADS * dq
    hrows = rows // DF_HEADS
    tri = _strict_upper_ones(page)

    def sb_copies(step, slot):
        out = []
        for g in range(group):
            pg = pt_ref[b, n_pages - 1 - ((step - 1) * group + g)]
            src = pl.ds(pl.multiple_of(pg * BRANCH_W, BRANCH_W), BRANCH_W)
            out.append(pltpu.make_async_copy(ksb_hbm.at[src, :], ksb_buf.at[slot, g], sb_sem.at[slot]))
            out.append(pltpu.make_async_copy(vsb_hbm.at[src, :], vsb_buf.at[slot, g], sb_sem.at[slot]))
        return out

    def sb_prefetch(step):
        go = (step < n_steps) & (jnp.max(car_s[...]) > -SB_STOP)
        sb_live[0] = go.astype(jnp.int32)

        @pl.when(go)
        def _():
            for cp in sb_copies(step, step % 2):
                cp.start()

    def sb_update(z_list, mask, v_list, v_transposed):
        carry = car_s[...]
        acc = acc_sb[...]
        for z, v in zip(z_list, v_list):
            log_keep, logit = _sb_scores(z, mask, tri)
            a = jnp.exp(logit + carry)
            if mask is not None:
                a = jnp.where(mask, a, 0.0)
            a = a.astype(BF16)
            acc = acc + (_nt_dot(a, v) if v_transposed else _dot(a, v))
            carry = carry + jnp.sum(log_keep, axis=1, keepdims=True)
        acc_sb[...] = acc
        car_s[...] = carry

    def df_update(s_list, v_of):
        m_old = m_s[...]
        m_new = m_old
        for sc in s_list:
            m_new = jnp.maximum(m_new, jnp.max(sc, axis=1, keepdims=True))
        alpha = jnp.exp(m_old - m_new)
        p_list = [jnp.exp(sc - m_new) for sc in s_list]
        l_new = alpha * l_s[...]
        for p in p_list:
            l_new = l_new + jnp.sum(p, axis=1, keepdims=True)
        for h in range(DF_HEADS):
            sl = slice(h * hrows, (h + 1) * hrows)
            acc = alpha[sl] * acc_df[sl, :]
            for g, p in enumerate(p_list):
                acc = acc + _dot(p[sl].astype(BF16), v_of(g, h))
            acc_df[sl, :] = acc
        m_s[...] = m_new
        l_s[...] = l_new

    @pl.when(s == 0)
    def _():
        rq = lax.broadcasted_iota(jnp.int32, (rows, BRANCH_W), 0) // dq
        lq = lax.broadcasted_iota(jnp.int32, (rows, BRANCH_W), 1) // HEAD_DIM
        for c, dst in ((0, qsb_s), (3, qdf_s)):
            q = qkv_ref[0, :, c * BRANCH_W:(c + 1) * BRANCH_W].astype(F32)
            qt = jnp.concatenate([q] * SB_HEADS, axis=0)
            dst[...] = jnp.where(rq == lq, qt, 0.0).astype(BF16)
        for h in range(DF_HEADS):
            far = tab_ref[N_BUCKETS - 1, h]
            for t in range(2):
                bias_s[t, h * hrows:(h + 1) * hrows, :] = _bias_from_buckets(
                    bkt_ref[t, h * hrows:(h + 1) * hrows, :], lambda b: tab_ref[b, h] - far)
        acc_sb[...] = jnp.zeros_like(acc_sb)
        car_s[...] = jnp.zeros_like(car_s)
        acc_df[...] = jnp.zeros_like(acc_df)
        m_s[...] = jnp.full_like(m_s, NEG_BIG)
        l_s[...] = jnp.zeros_like(l_s)

        def new_rows(c):
            x = qkv_ref[0, :, c * BRANCH_W:(c + 1) * BRANCH_W].astype(F32)
            return jnp.concatenate([x, jnp.zeros((page - dq, BRANCH_W), F32)], axis=0).astype(BF16)

        row = lax.broadcasted_iota(jnp.int32, (rows, page), 0) % dq
        col = lax.broadcasted_iota(jnp.int32, (rows, page), 1)
        sb_update([_nt_dot(qsb_s[...], new_rows(1))], col < row, [new_rows(2)], False)
        v_new = new_rows(5)
        sc = jnp.where(col <= row, _nt_dot(qdf_s[...], new_rows(4)) + bias_s[0], NEG_BIG)
        df_update([sc], lambda g, h: v_new[:, h * LANES:(h + 1) * LANES])
        sb_prefetch(1)

    def cached_step(newest):
        @pl.when(sb_live[0] == 1)
        def _():
            slot = s % 2
            for cp in sb_copies(s, slot):
                cp.wait()
            q = qsb_s[...]
            sb_update([_dot(q, ksb_buf[slot, g].astype(BF16)) for g in range(group)], None,
                      [vsb_buf[slot, g].astype(BF16) for g in range(group)], True)
            sb_prefetch(s + 1)

        q = qdf_s[...]
        scores = [_dot(q, k[...].astype(BF16)) for k in kdf]
        if newest:
            scores[0] = scores[0] + bias_s[1]
        df_update(scores, lambda g, h: vdf[g][pl.ds(h, page, stride=DF_HEADS), :].astype(BF16))

    @pl.when(s == 1)
    def _():
        cached_step(True)

    @pl.when(s >= 2)
    def _():
        cached_step(False)

    @pl.when(s == n_steps - 1)
    def _():
        rq = lax.broadcasted_iota(jnp.int32, (rows, BRANCH_W), 0)
        lq = lax.broadcasted_iota(jnp.int32, (rows, BRANCH_W), 1)
        sb = jnp.where(rq // dq == lq // HEAD_DIM, acc_sb[...], 0.0)
        sb = functools.reduce(lambda a, b: a + b, [sb[g * dq:(g + 1) * dq, :] for g in range(SB_HEADS)])
        osb_ref[0] = sb.astype(osb_ref.dtype)

        lam = _lambda(lamv_ref, lambda_init)
        norm = acc_df[...] / l_s[...]
        g = g_ref[...]
        heads = []
        for h in range(DF_HEADS):
            o = norm[h * hrows:h * hrows + dq, :] - lam * norm[h * hrows + dq:(h + 1) * hrows, :]
            heads.append(_rms(o, g))
        odf_ref[0] = (jnp.concatenate(heads, axis=1) * (1.0 - lambda_init)).astype(odf_ref.dtype)


def _decode(page_table, tab, lamv, qkv3, caches, dec_buckets, g_subln, lambda_init, page):
    nb, dq, _ = qkv3.shape
    n_pages = page_table.shape[1]
    group = PAGES_PER_STEP
    while n_pages % group:
        group //= 2
    rows = SB_HEADS * dq

    def cache_spec(g):
        def index(b, s, pt):
            return (pt[b, n_pages - 1 - (jnp.maximum(s - 1, 0) * group + g)], 0)
        return pl.BlockSpec((BRANCH_W, page), index)

    cache_specs = [pl.BlockSpec(memory_space=pl.ANY)] * 2 + [cache_spec(g) for _ in range(2) for g in range(group)]
    cache_args = list(caches[:2]) + [c for c in caches[2:] for _ in range(group)]
    out = jax.ShapeDtypeStruct((nb, dq, BRANCH_W), BF16)
    const2 = lambda b, s, pt: (0, 0)
    grid_spec = pltpu.PrefetchScalarGridSpec(
        num_scalar_prefetch=1,
        grid=(nb, n_pages // group + 1),
        in_specs=[pl.BlockSpec(memory_space=pltpu.SMEM),
                  pl.BlockSpec((4, HEAD_DIM), const2),
                  pl.BlockSpec((1, dq, SAMPLE_QKV_W), lambda b, s, pt: (b, 0, 0))]
                 + cache_specs
                 + [pl.BlockSpec((2, rows, page), lambda b, s, pt: (0, 0, 0)),
                    pl.BlockSpec((1, LANES), const2)],
        out_specs=[pl.BlockSpec((1, dq, BRANCH_W), lambda b, s, pt: (b, 0, 0))] * 2,
        scratch_shapes=[pltpu.VMEM((rows, BRANCH_W), BF16), pltpu.VMEM((rows, BRANCH_W), BF16),
                        pltpu.VMEM((2, rows, page), F32),
                        pltpu.VMEM((rows, BRANCH_W), F32), pltpu.VMEM((rows, 1), F32),
                        pltpu.VMEM((rows, LANES), F32), pltpu.VMEM((rows, 1), F32),
                        pltpu.VMEM((rows, 1), F32),
                        pltpu.VMEM((2, group, BRANCH_W, page), F32), pltpu.VMEM((2, group, BRANCH_W, page), F32),
                        pltpu.SemaphoreType.DMA((2,)), pltpu.SMEM((1,), jnp.int32)],
    )
    return pl.pallas_call(
        functools.partial(_decode_kernel, dq=dq, page=page, lambda_init=lambda_init, group=group),
        grid_spec=grid_spec,
        out_shape=[out, out],
        compiler_params=pltpu.CompilerParams(dimension_semantics=("arbitrary", "arbitrary"),
                                             vmem_limit_bytes=VMEM_LIMIT),
        name="decode_attn",
    )(page_table, tab, lamv, qkv3, *cache_args, dec_buckets, g_subln)


def _post_kernel(sb_ref, df_ref, ga_ref, gb_ref, x_ref, bg_ref, wa_ref, wb_ref, wo_ref, gf_ref,
                 wrt_ref, brt_ref, cnt0_ref, x1_ref, h2_ref, comb_ref, pos_ref, cnt_ref, run_s):
    @pl.when(pl.program_id(0) == 0)
    def _():
        run_s[...] = cnt0_ref[...]

    a = _dot(sb_ref[...], wa_ref[...])
    b = _dot(df_ref[...], wb_ref[...])
    merged = jax.nn.sigmoid(ga_ref[...] + bg_ref[0:1, :]) * a + jax.nn.sigmoid(gb_ref[...] + bg_ref[1:2, :]) * b
    x1 = x_ref[...] + _dot(merged.astype(BF16), wo_ref[...])
    x1_ref[...] = x1
    h2 = _rms(x1, gf_ref[...])
    h2_ref[...] = h2.astype(BF16)

    wr = wrt_ref[...]
    wr_hi = wr.astype(BF16)
    wr_lo = (wr - wr_hi.astype(F32)).astype(BF16)
    h_hi = h2.astype(BF16)
    h_lo = (h2 - h_hi.astype(F32)).astype(BF16)
    logits = _nt_dot(wr_hi, h_hi) + _nt_dot(wr_lo, h_hi) + _nt_dot(wr_hi, h_lo) + brt_ref[...]

    expert = lax.broadcasted_iota(jnp.int32, logits.shape, 0).astype(F32)
    vals, hots = [], []
    for _ in range(TOP_K):
        mx = jnp.max(logits, axis=0, keepdims=True)
        first = jnp.min(jnp.where(logits == mx, expert, float(N_EXPERTS)), axis=0, keepdims=True)
        hot = expert == first
        vals.append(mx)
        hots.append(hot)
        logits = jnp.where(hot, -jnp.inf, logits)
    es = [jnp.exp(v - vals[0]) for v in vals]
    denom = functools.reduce(lambda u, v: u + v, es)
    comb = jnp.zeros(logits.shape, F32)
    routed = jnp.zeros(logits.shape, F32)
    for e, hot in zip(es, hots):
        comb = comb + jnp.where(hot, e / denom, 0.0)
        routed = routed + jnp.where(hot, 1.0, 0.0)
    comb_ref[...] = comb

    tm = logits.shape[1]
    j = lax.broadcasted_iota(jnp.int32, (tm, tm), 0)
    s = lax.broadcasted_iota(jnp.int32, (tm, tm), 1)
    earlier = jnp.where(j < s, 1.0, 0.0).astype(BF16)
    rank = _dot(routed.astype(BF16), earlier) + run_s[...]
    pos_ref[...] = jnp.where(routed > 0.0, rank, -1.0 - rank)
    run = run_s[...] + jnp.sum(routed, axis=1, keepdims=True)
    run_s[...] = run
    cnt_ref[...] = run


def _post(sb, df, gates, x, b_gate, wa, wb, wo, g_ffn, w_router_t, b_router_t, cnt0, tm):
    n = x.shape[0]
    row = lambda i: (i, 0)
    col = lambda i: (0, i)
    const = lambda i: (0, 0)
    routing = jax.ShapeDtypeStruct((N_EXPERTS, n), F32)
    return pl.pallas_call(
        _post_kernel,
        grid=(n // tm,),
        in_specs=[pl.BlockSpec((tm, BRANCH_W), row), pl.BlockSpec((tm, BRANCH_W), row),
                  pl.BlockSpec((tm, D_MODEL), lambda i: (i, 0)), pl.BlockSpec((tm, D_MODEL), lambda i: (i, 1)),
                  pl.BlockSpec((tm, D_MODEL), row),
                  pl.BlockSpec((2, D_MODEL), const),
                  pl.BlockSpec((BRANCH_W, D_MODEL), const), pl.BlockSpec((BRANCH_W, D_MODEL), const),
                  pl.BlockSpec((D_MODEL, D_MODEL), const),
                  pl.BlockSpec((1, D_MODEL), const),
                  pl.BlockSpec((N_EXPERTS, D_MODEL), const), pl.BlockSpec((N_EXPERTS, 1), const),
                  pl.BlockSpec((N_EXPERTS, 1), const)],
        out_specs=[pl.BlockSpec((tm, D_MODEL), row), pl.BlockSpec((tm, D_MODEL), row),
                   pl.BlockSpec((N_EXPERTS, tm), col), pl.BlockSpec((N_EXPERTS, tm), col),
                   pl.BlockSpec((N_EXPERTS, 1), const)],
        out_shape=[jax.ShapeDtypeStruct((n, D_MODEL), F32), jax.ShapeDtypeStruct((n, D_MODEL), BF16),
                   routing, routing, jax.ShapeDtypeStruct((N_EXPERTS, 1), F32)],
        scratch_shapes=[pltpu.VMEM((N_EXPERTS, 1), F32)],
        compiler_params=pltpu.CompilerParams(dimension_semantics=("arbitrary",),
                                             vmem_limit_bytes=VMEM_LIMIT),
        name="post_attn",
    )(sb, df, gates, gates, x, b_gate, wa, wb, wo, g_ffn, w_router_t, b_router_t, cnt0)


ROW_TILE = 256
TOKEN_TILE = 256
SLAB = 64
GATHER_ROWS = 128
GATHER_SPAN = 6
SLAB_ALIGN = 16


def _expert_rows_kernel(te_ref, q0_ref, wlo_ref, whi_ref, valid_ref,
                        h_ref, pos_ref, comb_ref, wi_ref, bi_ref, wo_ref, bo_ref, o_ref, x_s, g_s, *, span):
    r = pl.program_id(0)

    @pl.when(valid_ref[r] == 0)
    def _():
        o_ref[...] = jnp.zeros_like(o_ref)

    @pl.when(valid_ref[r] == 1)
    def _():
        e = te_ref[r]
        q0 = q0_ref[r].astype(F32)
        n_win = pos_ref.shape[1]
        parts = ROW_TILE // GATHER_ROWS
        lo = [wlo_ref[r * parts + p] for p in range(parts)]
        hi = [whi_ref[r * parts + p] for p in range(parts)]
        fits = functools.reduce(lambda a, b: a & b, [hi[p] - lo[p] <= span for p in range(parts)])

        def picks(rows, row0, w0, k):
            row = lax.broadcasted_iota(jnp.int32, (rows, TOKEN_TILE), 0).astype(F32) + (q0 + row0)
            hits = [pos_ref[e, pl.ds(w0 + i, 1), :] == row for i in range(k)]
            gsum = functools.reduce(lambda a, b: a + b,
                                    [jnp.where(h, comb_ref[e, pl.ds(w0 + i, 1), :], 0.0) for i, h in enumerate(hits)])
            return [jnp.where(h, 1.0, 0.0).astype(BF16) for h in hits], gsum

        @pl.when(fits)
        def _():
            for p in range(parts):
                rows = slice(p * GATHER_ROWS, (p + 1) * GATHER_ROWS)
                w0 = jnp.minimum(lo[p], n_win - span)
                onehot, gsum = picks(GATHER_ROWS, p * GATHER_ROWS, w0, span)
                tok0 = pl.multiple_of(w0 * TOKEN_TILE, TOKEN_TILE)
                x_s[rows, :] = _dot(jnp.concatenate(onehot, axis=1), h_ref[pl.ds(tok0, span * TOKEN_TILE), :])
                g_s[rows, :] = gsum

        @pl.when(jnp.logical_not(fits))
        def _():
            x_s[...] = jnp.zeros_like(x_s)
            g_s[...] = jnp.zeros_like(g_s)

            def window(w, c):
                onehot, gsum = picks(ROW_TILE, 0, w, 1)
                x_s[...] += _dot(onehot[0], h_ref[pl.ds(pl.multiple_of(w * TOKEN_TILE, TOKEN_TILE), TOKEN_TILE), :])
                g_s[...] += gsum
                return c

            lax.fori_loop(lo[0], hi[parts - 1], window, 0)

        h = x_s[...].astype(BF16)
        gate = jnp.minimum(_dot(h, wi_ref[0, :, :D_FF]) + bi_ref[0, :, :D_FF], SWIGLU_LIMIT)
        up = jnp.clip(_dot(h, wi_ref[0, :, D_FF:]) + bi_ref[0, :, D_FF:], -SWIGLU_LIMIT, SWIGLU_LIMIT)
        act = (up + 1.0) * gate * jax.nn.sigmoid(SWIGLU_ALPHA * gate)
        weight = jnp.sum(g_s[...], axis=1, keepdims=True)
        o_ref[...] = (weight * (_dot(act.astype(BF16), wo_ref[0]) + bo_ref[0])).astype(o_ref.dtype)


def _expert_rows(tile_meta, h2, pos3, comb3, wi, bi, wo, bo, n_row_tiles):
    n = h2.shape[0]
    once = pl.Buffered(1)
    exp3 = lambda r, te, *_: (te[r], 0, 0)
    span = min(GATHER_SPAN, pos3.shape[1])
    grid_spec = pltpu.PrefetchScalarGridSpec(
        num_scalar_prefetch=5,
        grid=(n_row_tiles,),
        in_specs=[pl.BlockSpec((n, D_MODEL), lambda r, *_: (0, 0), pipeline_mode=once),
                  pl.BlockSpec(pos3.shape, lambda r, *_: (0, 0, 0), pipeline_mode=once),
                  pl.BlockSpec(comb3.shape, lambda r, *_: (0, 0, 0), pipeline_mode=once),
                  pl.BlockSpec((1, D_MODEL, 2 * D_FF), exp3, pipeline_mode=once),
                  pl.BlockSpec((1, 1, 2 * D_FF), exp3),
                  pl.BlockSpec((1, D_FF, D_MODEL), exp3, pipeline_mode=once),
                  pl.BlockSpec((1, 1, D_MODEL), exp3)],
        out_specs=pl.BlockSpec((ROW_TILE, D_MODEL), lambda r, *_: (r, 0)),
        scratch_shapes=[pltpu.VMEM((ROW_TILE, D_MODEL), F32), pltpu.VMEM((ROW_TILE, TOKEN_TILE), F32)],
    )
    return pl.pallas_call(
        functools.partial(_expert_rows_kernel, span=span),
        grid_spec=grid_spec,
        out_shape=jax.ShapeDtypeStruct((n_row_tiles * ROW_TILE, D_MODEL), BF16),
        compiler_params=pltpu.CompilerParams(dimension_semantics=("arbitrary",),
                                             vmem_limit_bytes=VMEM_LIMIT),
        name="expert_rows",
    )(*tile_meta, h2, pos3, comb3, wi, bi, wo, bo)


def _combine_kernel(start_ref, nchunk_ref, pos_ref, off_ref, x1_ref, gfin_ref, rows_hbm, y_ref,
                    buf, sem, *, tile0, last_start):
    t = pl.program_id(0) + tile0
    pad = jnp.zeros((LANES - N_EXPERTS, TOKEN_TILE), F32)
    rank = jnp.concatenate([pos_ref[...], pad], axis=0).T
    dest = jnp.where(rank >= 0.0, rank + off_ref[...], -1.0)
    lane = lax.broadcasted_iota(jnp.int32, (1, 2 * SLAB), 1).astype(F32)

    def chunk(c, acc):
        starts = [start_ref[t * N_EXPERTS + e] + c * SLAB for e in range(N_EXPERTS)]
        copies = []
        for e in range(N_EXPERTS):
            src = pl.multiple_of(jnp.minimum(starts[e], last_start), SLAB_ALIGN)
            cp = pltpu.make_async_copy(rows_hbm.at[pl.ds(src, SLAB), :], buf.at[pl.ds(e * SLAB, SLAB), :], sem)
            cp.start()
            copies.append(cp)
        pieces = []
        for e in range(0, N_EXPERTS, 2):
            rel = []
            for k in range(2):
                d = dest[:, e + k:e + k + 1] - starts[e + k].astype(F32)
                rel.append(jnp.where((d >= 0.0) & (d < SLAB), d + k * SLAB, -1.0))
            hit = (rel[0] == lane) | (rel[1] == lane)
            pieces.append(jnp.where(hit, 1.0, 0.0).astype(BF16))
        select = jnp.concatenate(pieces, axis=1)
        for cp in copies:
            cp.wait()
        return acc + _dot(select, buf[...])

    acc = lax.fori_loop(0, nchunk_ref[t], chunk, jnp.zeros((TOKEN_TILE, D_MODEL), F32))
    y_ref[...] = _rms(x1_ref[...] + acc, gfin_ref[...])


def _combine(slab_start, n_chunks, pos_t, off_row, x1, g_final, rows, tile0):
    n = x1.shape[0]
    n_rows = rows.shape[0]
    grid_spec = pltpu.PrefetchScalarGridSpec(
        num_scalar_prefetch=2,
        grid=(n // TOKEN_TILE,),
        in_specs=[pl.BlockSpec((N_EXPERTS, TOKEN_TILE), lambda i, *_: (0, i + tile0)),
                  pl.BlockSpec((1, LANES), lambda i, *_: (0, 0)),
                  pl.BlockSpec((TOKEN_TILE, D_MODEL), lambda i, *_: (i, 0)),
                  pl.BlockSpec((1, D_MODEL), lambda i, *_: (0, 0)),
                  pl.BlockSpec(memory_space=pl.ANY)],
        out_specs=pl.BlockSpec((TOKEN_TILE, D_MODEL), lambda i, *_: (i, 0)),
        scratch_shapes=[pltpu.VMEM((N_EXPERTS * SLAB, D_MODEL), BF16), pltpu.SemaphoreType.DMA(())],
    )
    return pl.pallas_call(
        functools.partial(_combine_kernel, tile0=tile0, last_start=n_rows - SLAB),
        grid_spec=grid_spec,
        out_shape=jax.ShapeDtypeStruct((n, D_MODEL), F32),
        compiler_params=pltpu.CompilerParams(dimension_semantics=("arbitrary",),
                                             vmem_limit_bytes=VMEM_LIMIT),
        name="combine",
    )(slab_start, n_chunks, pos_t, off_row, x1, g_final, rows)


def _routing_tables(pos_t, counts, n_tokens):
    n_win = n_tokens // TOKEN_TILE
    max_tiles = (TOP_K * n_tokens + N_EXPERTS * (ROW_TILE - 1)) // ROW_TILE
    n_row_tiles = max_tiles + 1
    counts = counts.astype(jnp.int32)
    tiles_e = (counts + ROW_TILE - 1) // ROW_TILE
    tile_end = jnp.cumsum(tiles_e)
    tile_start = tile_end - tiles_e
    off = tile_start * ROW_TILE
    r = jnp.arange(n_row_tiles, dtype=jnp.int32)
    te = jnp.minimum(jnp.sum(r[:, None] >= tile_end[None, :], axis=1), N_EXPERTS - 1).astype(jnp.int32)
    valid = (r < tile_end[-1]).astype(jnp.int32)
    mine = (te[:, None] == jnp.arange(N_EXPERTS, dtype=jnp.int32)[None, :]).astype(jnp.int32)
    q0 = (r - jnp.sum(mine * tile_start[None, :], axis=1)) * ROW_TILE
    first = pos_t[:, ::TOKEN_TILE]
    before = jnp.where(first < 0, -1.0 - first, first).astype(jnp.int32)
    edges = jnp.concatenate([before, counts[:, None]], axis=1)
    edges_r = jnp.sum(mine[:, :, None] * edges[None, :, :], axis=1)
    e_lo, e_hi = edges_r[:, :-1], edges_r[:, 1:]
    part_q0 = q0[:, None] + jnp.arange(0, ROW_TILE, GATHER_ROWS, dtype=jnp.int32)[None, :]
    wlo = jnp.sum(e_hi[:, None, :] <= part_q0[:, :, None], axis=2).astype(jnp.int32) * valid[:, None]
    whi = jnp.sum(e_lo[:, None, :] < (part_q0 + GATHER_ROWS)[:, :, None], axis=2).astype(jnp.int32) * valid[:, None]
    wlo, whi = wlo.reshape(-1), whi.reshape(-1)
    start = off[:, None] + before
    start_al = (start // SLAB_ALIGN) * SLAB_ALIGN
    need = start - start_al + (edges[:, 1:] - edges[:, :-1])
    n_chunks = jnp.maximum(jnp.max((need + SLAB - 1) // SLAB, axis=0), 1).astype(jnp.int32)
    slab_start = start_al.T.reshape(-1).astype(jnp.int32)
    off_row = jnp.pad(off.astype(F32), (0, LANES - N_EXPERTS))[None, :]
    return (te, q0.astype(jnp.int32), wlo, whi, valid), n_row_tiles, slab_start, n_chunks, off_row


def _bucket_np(n):
    max_exact = N_BUCKETS // 2
    nf = np.maximum(n, max_exact).astype(np.float32)
    large = max_exact + (np.log(nf / np.float32(max_exact)) / np.float32(math.log(MAX_DISTANCE / max_exact))
                         * np.float32(N_BUCKETS - max_exact)).astype(np.int32)
    return np.where(n < max_exact, n, np.minimum(large, N_BUCKETS - 1)).astype(np.int32)


def _tile(n, cap):
    t = cap
    while n % t:
        t //= 2
    return t


def kernel(x_prompt, x_sample, cache_sb_k, cache_sb_v, cache_df_k, cache_df_v, page_table, rel_bias, g_attn, w_in, b_gate, df_lambda_q1, df_lambda_k1, df_lambda_q2, df_lambda_k2, g_subln, w_proj_a, w_proj_b, w_out, g_ffn, w_router, b_router, w_exp_in, b_exp_in, w_exp_out, b_exp_out, g_final):
    depth = w_in.shape[0]
    assert depth == 1, "single-layer step"
    batch, seq, _ = x_prompt.shape
    nb, dq, _ = x_sample.shape
    n_pool, page = cache_sb_k.shape[1], cache_sb_k.shape[2]
    tq = _tile(seq, 256)
    assert tq >= MAX_DISTANCE and page >= MAX_DISTANCE, "bias is constant beyond the two nearest tiles"
    lambda_init = 0.8 - 0.6 * math.exp(-0.3 * 0)

    w = w_in[0]
    sec = lambda c: w[:, c * BRANCH_W:(c + 1) * BRANCH_W]
    w_in_b = w.astype(BF16)
    w_rows = jnp.concatenate([sec(0), sec(1), sec(2), sec(4), sec(5), w[:, 6 * BRANCH_W:]], axis=1).astype(BF16)
    w_cols = jnp.concatenate([sec(1), sec(2), sec(4), sec(3), sec(5)], axis=1).T.astype(BF16)
    wa, wb, wo = w_proj_a[0].astype(BF16), w_proj_b[0].astype(BF16), w_out[0].astype(BF16)
    wi, wo_e = w_exp_in[0].astype(BF16), w_exp_out[0].astype(BF16)
    bi, bo = b_exp_in[0][:, None, :], b_exp_out[0][:, None, :]
    lamv = jnp.stack([df_lambda_q1[0], df_lambda_k1[0], df_lambda_q2[0], df_lambda_k2[0]]).astype(F32)
    g_sub = g_subln[0][None, :]
    g_a, g_f, g_fin = g_attn[0][None, :], g_ffn[0][None, :], g_final[None, :]
    b_r = b_router[0][None, :]

    k_i = np.arange(tq)[:, None]
    q_i = np.arange(tq)[None, :]
    near_buckets = jnp.asarray(_bucket_np(np.stack([np.maximum(q_i - k_i, 0), q_i - k_i + tq])))
    r_i = (np.arange(SB_HEADS * dq) % dq)[:, None]
    c_i = np.arange(page)[None, :]
    dec_buckets = jnp.asarray(_bucket_np(np.stack([np.maximum(r_i - c_i, 0), r_i + page - c_i])))

    xp = x_prompt.reshape(batch * seq, D_MODEL)
    xs = x_sample.reshape(nb * dq, D_MODEL)

    qkv_p, dv_p, gates_p, skt, svt, dkt, dqt, dvt = _inproj_prompt(xp, g_a, w_rows, w_cols, batch, seq, tq)
    qkv_s, sk_s, sv_s, dk_s, dv_s, gates_s = _inproj_sample(xs, g_a, w_in_b, _tile(nb * dq, 256))

    sb_p = _sb_attn(qkv_p, batch, seq, tq)
    df_p = _df_attn(rel_bias, lamv, dqt, qkv_p, dvt, near_buckets, g_sub, batch, seq, tq, lambda_init)

    rows_of_pool = n_pool * BRANCH_W
    caches = [jnp.transpose(cache_sb_k[0], (0, 2, 3, 1)).reshape(rows_of_pool, page),
              jnp.transpose(cache_sb_v[0], (0, 2, 3, 1)).reshape(rows_of_pool, page),
              jnp.transpose(cache_df_k[0], (0, 2, 3, 4, 1)).reshape(rows_of_pool, page),
              cache_df_v[0].reshape(rows_of_pool, LANES)]
    sb_s, df_s = _decode(page_table, rel_bias, lamv, qkv_s.reshape(nb, dq, SAMPLE_QKV_W), caches, dec_buckets,
                         g_sub, lambda_init, page)
    sb_s = sb_s.reshape(nb * dq, BRANCH_W)
    df_s = df_s.reshape(nb * dq, BRANCH_W)

    n_p, n_s = batch * seq, nb * dq
    assert n_p % TOKEN_TILE == 0 and n_s % TOKEN_TILE == 0
    w_rt, b_rt = w_router[0].T, b_router[0][:, None]
    post = functools.partial(_post, b_gate=b_gate[0], wa=wa, wb=wb, wo=wo, g_ffn=g_f, w_router_t=w_rt, b_router_t=b_rt)
    x1_p, h2_p, comb_p, pos_p, cnt_p = post(sb_p, df_p, gates_p, xp, cnt0=jnp.zeros((N_EXPERTS, 1), F32),
                                            tm=_tile(n_p, 512))
    x1_s, h2_s, comb_s, pos_s, cnt_all = post(sb_s, df_s, gates_s, xs, cnt0=cnt_p, tm=_tile(n_s, 512))
    h2 = jnp.concatenate([h2_p, h2_s], axis=0)
    comb_t = jnp.concatenate([comb_p, comb_s], axis=1)
    pos_t = jnp.concatenate([pos_p, pos_s], axis=1)
    n_all = n_p + n_s
    tile_meta, n_row_tiles, slab_start, n_chunks, off_row = _routing_tables(pos_t, cnt_all[:, 0], n_all)
    by_window = lambda a: a.reshape(N_EXPERTS, n_all // TOKEN_TILE, TOKEN_TILE)
    rows = _expert_rows(tile_meta, h2, by_window(pos_t), by_window(comb_t), wi, bi, wo_e, bo, n_row_tiles)
    y_p = _combine(slab_start, n_chunks, pos_t, off_row, x1_p, g_fin, rows, 0)
    y_s = _combine(slab_start, n_chunks, pos_t, off_row, x1_s, g_fin, rows, n_p // TOKEN_TILE)
    y_p = y_p.reshape(batch, seq, D_MODEL)
    y_s = y_s.reshape(nb, dq, D_MODEL)

    def from_t(a, feat_shape):
        a = a.reshape((batch,) + feat_shape + (seq,))
        return jnp.moveaxis(a, -1, 1)[None]

    s_lead = (depth, nb, dq)
    return (y_p, y_s,
            from_t(skt, (SB_HEADS, HEAD_DIM)), from_t(svt, (SB_HEADS, HEAD_DIM)),
            from_t(dkt, (DF_HEADS, 2, HEAD_DIM)), dv_p.reshape(depth, batch, seq, DF_HEADS, 2 * HEAD_DIM),
            sk_s.reshape(s_lead + (SB_HEADS, HEAD_DIM)), sv_s.reshape(s_lead + (SB_HEADS, HEAD_DIM)),
            dk_s.reshape(s_lead + (DF_HEADS, 2, HEAD_DIM)), dv_s.reshape(s_lead + (DF_HEADS, 2 * HEAD_DIM)))
```
